```python
import math
import jax, jax.numpy as jnp
from jax import lax
import numpy as np

D_MODEL = 2048
BATCH = 8
SEQ = 2048
DEPTH = 4

GRID_W = 64
CTX_LEN = 256

GLA_HEADS = 4
GLA_DK = 128
GLA_DV = 256
GLA_QK = GLA_HEADS * GLA_DK
GLA_V = GLA_HEADS * GLA_DV
GLA_GATE_RANK = 16
GLA_GATE_TAU = 16.0
GLA_CHUNK = 64
S5_WIDTH = 768
S5_GROUP = 16
S5_GROUPS = S5_WIDTH // S5_GROUP
S5_STATE = 64
ATTN_Q_HEADS = 8
ATTN_KV_HEADS = 2
ATTN_HEAD_DIM = 128
ATTN_Q = ATTN_Q_HEADS * ATTN_HEAD_DIM
ATTN_KV = ATTN_KV_HEADS * ATTN_HEAD_DIM
ATTN_BLOCK = 128
ROPE_THETA = 10000.0
D_FF = ((8 * D_MODEL + 3 * 256 - 1) // (3 * 256)) * 256
IN_SPLITS = (GLA_QK, GLA_QK, GLA_V, GLA_V, GLA_GATE_RANK, S5_WIDTH, ATTN_Q, ATTN_KV, ATTN_KV, 3 * D_MODEL)
IN_WIDTH = 2 * GLA_QK + 2 * GLA_V + GLA_GATE_RANK + S5_WIDTH + ATTN_Q + 2 * ATTN_KV + 3 * D_MODEL
DN_ALPHA = (2 * DEPTH) ** 0.25
DN_BETA = (8 * DEPTH) ** -0.25
EPS = 1e-6

kernel_name = "hybrid_gla_s5_gqa_prefix_dit"

F32 = jnp.float32


def layer_norm(x, w, b):
    xf = x.astype(F32)
    mu = jnp.mean(xf, -1, keepdims=True)
    var = jnp.mean(jnp.square(xf - mu), -1, keepdims=True)
    return ((xf - mu) * lax.rsqrt(var + EPS) * w + b).astype(x.dtype)


def rms_norm(x, w):
    xf = x.astype(F32)
    return (xf * lax.rsqrt(jnp.mean(xf * xf, -1, keepdims=True) + EPS) * w).astype(x.dtype)


def modulate(h, shift, scale):
    return h * (1 + scale) + shift


def post_norm(x, y, w, b):
    return layer_norm(DN_ALPHA * x + y, w, b)


def axial_rope(n_rows):
    rows = jnp.repeat(jnp.arange(n_rows), GRID_W).astype(F32)
    cols = jnp.tile(jnp.arange(GRID_W), n_rows).astype(F32)
    n_freq = ATTN_HEAD_DIM // 4
    inv = ROPE_THETA ** (-jnp.arange(n_freq, dtype=F32) / n_freq)
    ang = jnp.concatenate([rows[:, None] * inv, cols[:, None] * inv], -1)
    return jnp.cos(ang)[:, None, :], jnp.sin(ang)[:, None, :]


def apply_rope(x, cos, sin):
    xf = x.astype(F32).reshape(x.shape[:-1] + (x.shape[-1] // 2, 2))
    x0, x1 = xf[..., 0], xf[..., 1]
    out = jnp.stack([x0 * cos - x1 * sin, x0 * sin + x1 * cos], -1)
    return out.reshape(x.shape).astype(x.dtype)


def attend(q, k, v):
    s = jnp.einsum('bqhgd,bkhd->bhgqk', q, k, preferred_element_type=F32) * (ATTN_HEAD_DIM ** -0.5)
    p = jax.nn.softmax(s, axis=-1).astype(v.dtype)
    return jnp.einsum('bhgqk,bkhd->bqhgd', p, v)


def gqa_blocks(q, k, v):
    bsz, t = q.shape[:2]
    grp = ATTN_Q_HEADS // ATTN_KV_HEADS
    qb = q.reshape(bsz, t // ATTN_BLOCK, ATTN_BLOCK, ATTN_KV_HEADS, grp, ATTN_HEAD_DIM).swapaxes(0, 1)
    ob = lax.map(lambda qq: attend(qq, k, v), qb)
    return ob.swapaxes(0, 1).reshape(bsz, t, ATTN_Q)


def gla_chunked(q, k, v, log_a, s0):
    bsz, t, nh, _ = q.shape
    dv = v.shape[-1]
    n = t // GLA_CHUNK
    mask = jnp.tril(jnp.ones((GLA_CHUNK, GLA_CHUNK), bool))

    def to_chunks(z):
        return z.reshape(bsz, n, GLA_CHUNK, nh, z.shape[-1]).swapaxes(0, 1)

    def step(s, inp):
        qc, kc, vc, ac = inp
        b = jnp.cumsum(ac, axis=1)
        qe = qc * jnp.exp(b)
        ke = kc * jnp.exp(-b)
        att = jnp.where(mask, jnp.einsum('bihd,bjhd->bhij', qe, ke), 0.0)
        o = jnp.einsum('bhij,bjhv->bihv', att, vc) + jnp.einsum('bihd,bhdv->bihv', qe, s)
        b_last = b[:, -1]
        kd = kc * jnp.exp(b_last[:, None] - b)
        s = jnp.exp(b_last)[..., None] * s + jnp.einsum('bjhd,bjhv->bhdv', kd, vc)
        return s, o

    s_fin, o = lax.scan(step, s0, (to_chunks(q), to_chunks(k), to_chunks(v), to_chunks(log_a)))
    return o.swapaxes(0, 1).reshape(bsz, t, nh, dv), s_fin


def gla_bidirectional(q, k, v, glr, w_gate, b_gate, n_ctx):
    q, k, v = q.astype(F32), k.astype(F32), v.astype(F32)
    bsz, n_tok, nh, dk = q.shape
    dv = v.shape[-1]
    outs = []
    for d in range(2):
        log_a = jax.nn.log_sigmoid((glr @ w_gate[d] + b_gate[d]).astype(F32)) / GLA_GATE_TAU
        log_a = log_a.reshape(bsz, n_tok, nh, dk)
        ctx_in = [z[:, :n_ctx] for z in (q, k, v, log_a)]
        lat_in = [z[:, n_ctx:] for z in (q, k, v, log_a)]
        if d == 1:
            ctx_in = [jnp.flip(z, 1) for z in ctx_in]
            lat_in = [jnp.flip(z, 1) for z in lat_in]
        s0 = jnp.zeros((bsz, nh, dk, dv), F32)
        o_c, s_c = gla_chunked(*ctx_in, s0)
        o_l, _ = gla_chunked(*lat_in, s_c)
        if d == 1:
            o_c, o_l = jnp.flip(o_c, 1), jnp.flip(o_l, 1)
        outs.append(jnp.concatenate([o_c, o_l], 1))
    return outs[0] + outs[1]


def s5_discretise(lam_re, lam_im, log_dt):
    lam_re, lam_im = lam_re.astype(F32), lam_im.astype(F32)
    dt = jnp.exp(log_dt.astype(F32))[:, None]
    mag = jnp.exp(lam_re * dt)
    a_re, a_im = mag * jnp.cos(lam_im * dt), mag * jnp.sin(lam_im * dt)
    den = lam_re * lam_re + lam_im * lam_im
    nr, ni = a_re - 1, a_im
    k_re = (nr * lam_re + ni * lam_im) / den
    k_im = (ni * lam_re - nr * lam_im) / den
    return a_re, a_im, k_re, k_im


def s5_scan(bu_re, bu_im, a_re, a_im, s0_re, s0_im, reverse):
    idx = -1 if reverse else 0
    bu_re = bu_re.at[:, idx].add(a_re * s0_re - a_im * s0_im)
    bu_im = bu_im.at[:, idx].add(a_re * s0_im + a_im * s0_re)
    t = bu_re.shape[1]
    ar = jnp.broadcast_to(a_re, (1, t) + a_re.shape)
    ai = jnp.broadcast_to(a_im, (1, t) + a_im.shape)

    def combine(e1, e2):
        a1r, a1i, b1r, b1i = e1
        a2r, a2i, b2r, b2i = e2
        return (a2r * a1r - a2i * a1i, a2r * a1i + a2i * a1r,
                a2r * b1r - a2i * b1i + b2r, a2r * b1i + a2i * b1r + b2i)

    _, _, xr, xi = lax.associative_scan(combine, (ar, ai, bu_re, bu_im), axis=1, reverse=reverse)
    return xr, xi


def s5_bidirectional(u, n_ctx, lam_re, lam_im, log_dt, b_re, b_im, c_re, c_im, d_skip):
    bsz, n_tok, _ = u.shape
    uf = u.astype(F32).reshape(bsz, n_tok, S5_GROUPS, S5_GROUP)
    bu_re = jnp.einsum('btgc,gpc->btgp', uf, b_re.astype(F32))
    bu_im = jnp.einsum('btgc,gpc->btgp', uf, b_im.astype(F32))
    y = uf * d_skip.astype(F32).reshape(S5_GROUPS, S5_GROUP)
    zeros = jnp.zeros((bsz, S5_GROUPS, S5_STATE), F32)
    for d in range(2):
        rev = d == 1
        a_re, a_im, k_re, k_im = s5_discretise(lam_re[d], lam_im[d], log_dt[d])
        xr = k_re * bu_re - k_im * bu_im
        xi = k_re * bu_im + k_im * bu_re
        cr, ci = s5_scan(xr[:, :n_ctx], xi[:, :n_ctx], a_re, a_im, zeros, zeros, rev)
        end = 0 if rev else -1
        lr, li = s5_scan(xr[:, n_ctx:], xi[:, n_ctx:], a_re, a_im, cr[:, end], ci[:, end], rev)
        sr = jnp.concatenate([cr, lr], 1)
        si = jnp.concatenate([ci, li], 1)
        y = y + jnp.einsum('btgp,gcp->btgc', sr, c_re[d].astype(F32)) \
              - jnp.einsum('btgp,gcp->btgc', si, c_im[d].astype(F32))
    return y.reshape(bsz, n_tok, S5_WIDTH)


def token_mixers(h, n_ctx, keep_ctx, cos, sin, w_in, w_gla_gate, b_gla_gate, gla_norm_w,
                 s5_lam_re, s5_lam_im, s5_log_dt, s5_b_re, s5_b_im, s5_c_re, s5_c_im, s5_d,
                 w_s5_glu, q_norm_w, k_norm_w, w_proj_gla, w_proj_s5, w_proj_attn, w_out):
    bsz, n_tok, _ = h.shape
    cuts = [int(s) for s in np.cumsum(IN_SPLITS)[:-1]]
    gq, gk, gv, gr, glr, su, aq, ak, av, bg = jnp.split(h @ w_in, cuts, axis=-1)
    lo = 0 if keep_ctx else n_ctx

    def heads(z, nh):
        return z.reshape(bsz, n_tok, nh, -1)

    o = gla_bidirectional(heads(gq, GLA_HEADS) * (GLA_DK ** -0.5), heads(gk, GLA_HEADS),
                          heads(gv, GLA_HEADS), glr, w_gla_gate, b_gla_gate, n_ctx)[:, lo:]
    mu = jnp.mean(o, -1, keepdims=True)
    var = jnp.mean(jnp.square(o - mu), -1, keepdims=True)
    o = (o - mu) * lax.rsqrt(var + EPS) * gla_norm_w.astype(F32).reshape(GLA_HEADS, GLA_DV)
    o_gla = o.reshape(bsz, n_tok - lo, GLA_V).astype(h.dtype) * jax.nn.silu(gr[:, lo:])

    y = s5_bidirectional(su, n_ctx, s5_lam_re, s5_lam_im, s5_log_dt, s5_b_re, s5_b_im,
                         s5_c_re, s5_c_im, s5_d)[:, lo:].astype(h.dtype)
    y = jax.nn.gelu(y)
    o_s5 = y * jax.nn.sigmoid(y @ w_s5_glu)

    q = rms_norm(heads(aq, ATTN_Q_HEADS), q_norm_w)
    k = rms_norm(heads(ak, ATTN_KV_HEADS), k_norm_w)
    v = heads(av, ATTN_KV_HEADS)
    q_lat = apply_rope(q[:, n_ctx:], cos, sin)
    k_lat = apply_rope(k[:, n_ctx:], cos, sin)
    k_all = jnp.concatenate([k_lat, k[:, :n_ctx]], 1)
    v_all = jnp.concatenate([v[:, n_ctx:], v[:, :n_ctx]], 1)
    o_attn = gqa_blocks(q_lat, k_all, v_all)
    if keep_ctx:
        grp = ATTN_Q_HEADS // ATTN_KV_HEADS
        q_c = q[:, :n_ctx].reshape(bsz, n_ctx, ATTN_KV_HEADS, grp, ATTN_HEAD_DIM)
        o_c = attend(q_c, k[:, :n_ctx], v[:, :n_ctx]).reshape(bsz, n_ctx, ATTN_Q)
        o_attn = jnp.concatenate([o_c, o_attn], 1)

    g_a, g_b, g_c = jnp.split(jax.nn.sigmoid(bg[:, lo:]), 3, axis=-1)
    merged = g_a * (o_gla @ w_proj_gla) + g_b * (o_s5 @ w_proj_s5) + g_c * (o_attn @ w_proj_attn)
    return merged @ w_out


def swiglu(h, w_ffn_in, w_ffn_out):
    a, b = jnp.split(h @ w_ffn_in, 2, axis=-1)
    return (jax.nn.silu(a) * b) @ w_ffn_out


def _fwd_setup_inputs(seed: int = 0) -> dict:
    key = jax.random.key(seed)
    keys = iter(jax.random.split(key, 40))

    def nrm(shape, std):
        return std * jax.random.normal(next(keys), shape, F32)

    L, D, G, P = DEPTH, D_MODEL, S5_GROUPS, S5_STATE
    lam_im = jnp.pi * jnp.arange(P, dtype=F32) + nrm((L, 2, G, P), 0.01)
    return {
        "x": nrm((BATCH, SEQ, D), 1.0),
        "c": nrm((BATCH, D), 1.0),
        "ctx": nrm((BATCH, CTX_LEN, D), 1.0),
        "c_ctx": nrm((D,), 1.0),
        "w_ada": nrm((L, D, 6 * D), 0.5 * D ** -0.5),
        "b_ada": nrm((L, 6 * D), 0.02),
        "w_in": nrm((L, D, IN_WIDTH), D ** -0.5),
        "w_gla_gate": nrm((L, 2, GLA_GATE_RANK, GLA_QK), GLA_GATE_RANK ** -0.5),
        "b_gla_gate": nrm((L, 2, GLA_QK), 0.1),
        "gla_norm_w": 1.0 + nrm((L, GLA_V), 0.02),
        "s5_lam_re": -0.5 + nrm((L, 2, G, P), 0.01),
        "s5_lam_im": lam_im,
        "s5_log_dt": jax.random.uniform(next(keys), (L, 2, G), F32, math.log(1e-3), math.log(1e-1)),
        "s5_b_re": nrm((L, G, P, S5_GROUP), (2 * S5_GROUP) ** -0.5),
        "s5_b_im": nrm((L, G, P, S5_GROUP), (2 * S5_GROUP) ** -0.5),
        "s5_c_re": nrm((L, 2, G, S5_GROUP, P), 0.5),
        "s5_c_im": nrm((L, 2, G, S5_GROUP, P), 0.5),
        "s5_d": nrm((L, S5_WIDTH), 1.0),
        "w_s5_glu": nrm((L, S5_WIDTH, S5_WIDTH), S5_WIDTH ** -0.5),
        "q_norm_w": 1.0 + nrm((L, ATTN_HEAD_DIM), 0.02),
        "k_norm_w": 1.0 + nrm((L, ATTN_HEAD_DIM), 0.02),
        "w_proj_gla": nrm((L, GLA_V, D), GLA_V ** -0.5),
        "w_proj_s5": nrm((L, S5_WIDTH, D), S5_WIDTH ** -0.5),
        "w_proj_attn": nrm((L, ATTN_Q, D), ATTN_Q ** -0.5),
        "w_out": nrm((L, D, D), DN_BETA * D ** -0.5),
        "ln1_w": 1.0 + nrm((L, D), 0.02),
        "ln1_b": nrm((L, D), 0.02),
        "ln2_w": 1.0 + nrm((L, D), 0.02),
        "ln2_b": nrm((L, D), 0.02),
        "w_ffn_in": nrm((L, D, 2 * D_FF), D ** -0.5),
        "w_ffn_out": nrm((L, D_FF, D), DN_BETA * D_FF ** -0.5),
    }


def _fwd_reference(x, c, ctx, c_ctx, w_ada, b_ada, w_in, w_gla_gate, b_gla_gate, gla_norm_w,
              s5_lam_re, s5_lam_im, s5_log_dt, s5_b_re, s5_b_im, s5_c_re, s5_c_im, s5_d,
              w_s5_glu, q_norm_w, k_norm_w, w_proj_gla, w_proj_s5, w_proj_attn, w_out,
              ln1_w, ln1_b, ln2_w, ln2_b, w_ffn_in, w_ffn_out):
    n_ctx = ctx.shape[1]
    n_lat = x.shape[1]
    rows = n_lat // GRID_W
    cos, sin = axial_rope(rows)
    xc = ctx
    silu_c = jax.nn.silu(c)
    silu_cc = jax.nn.silu(c_ctx)
    for l in range(DEPTH):
        keep_ctx = l < DEPTH - 1
        mod = (silu_c @ w_ada[l] + b_ada[l])[:, None, :]
        mod_c = silu_cc @ w_ada[l] + b_ada[l]
        sh1, sc1, g1, sh2, sc2, g2 = jnp.split(mod, 6, axis=-1)
        csh1, csc1, cg1, csh2, csc2, cg2 = jnp.split(mod_c, 6, axis=-1)
        h = jnp.concatenate([modulate(xc, csh1, csc1), modulate(x, sh1, sc1)], axis=1)
        mix = token_mixers(h, n_ctx, keep_ctx, cos, sin, w_in[l], w_gla_gate[l], b_gla_gate[l],
                           gla_norm_w[l], s5_lam_re[l], s5_lam_im[l], s5_log_dt[l], s5_b_re[l],
                           s5_b_im[l], s5_c_re[l], s5_c_im[l], s5_d[l], w_s5_glu[l], q_norm_w[l],
                           k_norm_w[l], w_proj_gla[l], w_proj_s5[l], w_proj_attn[l], w_out[l])
        mix_lat = mix[:, mix.shape[1] - n_lat:]
        x = post_norm(x, g1 * mix_lat, ln1_w[l], ln1_b[l])
        x = post_norm(x, g2 * swiglu(modulate(x, sh2, sc2), w_ffn_in[l], w_ffn_out[l]), ln2_w[l], ln2_b[l])
        if keep_ctx:
            xc = post_norm(xc, cg1 * mix[:, :n_ctx], ln1_w[l], ln1_b[l])
            xc = post_norm(xc, cg2 * swiglu(modulate(xc, csh2, csc2), w_ffn_in[l], w_ffn_out[l]),
                           ln2_w[l], ln2_b[l])
    return x


import jax as _jax
import jax.numpy as _jnp

TWIN_FORMAT = 'train_step'
FWD_PARAMS = ['x', 'c', 'ctx', 'c_ctx', 'w_ada', 'b_ada', 'w_in', 'w_gla_gate', 'b_gla_gate', 'gla_norm_w', 's5_lam_re', 's5_lam_im', 's5_log_dt', 's5_b_re', 's5_b_im', 's5_c_re', 's5_c_im', 's5_d', 'w_s5_glu', 'q_norm_w', 'k_norm_w', 'w_proj_gla', 'w_proj_s5', 'w_proj_attn', 'w_out', 'ln1_w', 'ln1_b', 'ln2_w', 'ln2_b', 'w_ffn_in', 'w_ffn_out']
TWIN_WEIGHTS = ['c_ctx', 'w_ada', 'b_ada', 'w_in', 'w_gla_gate', 'b_gla_gate', 'gla_norm_w', 's5_lam_re', 's5_lam_im', 's5_log_dt', 's5_b_re', 's5_b_im', 's5_c_re', 's5_c_im', 's5_d', 'w_s5_glu', 'q_norm_w', 'k_norm_w', 'w_proj_gla', 'w_proj_s5', 'w_proj_attn', 'w_out', 'ln1_w', 'ln1_b', 'ln2_w', 'ln2_b', 'w_ffn_in', 'w_ffn_out']
TWIN_DIFF_INPUT = 'x'
TWIN_INPUTS = ['x', 'c', 'ctx', 'c_ctx', 'w_ada', 'b_ada', 'w_in', 'w_gla_gate', 'b_gla_gate', 'gla_norm_w', 's5_lam_re', 's5_lam_im', 's5_log_dt', 's5_b_re', 's5_b_im', 's5_c_re', 's5_c_im', 's5_d', 'w_s5_glu', 'q_norm_w', 'k_norm_w', 'w_proj_gla', 'w_proj_s5', 'w_proj_attn', 'w_out', 'ln1_w', 'ln1_b', 'ln2_w', 'ln2_b', 'w_ffn_in', 'w_ffn_out', 'loss_target', 'm_c_ctx', 'm_w_ada', 'm_b_ada', 'm_w_in', 'm_w_gla_gate', 'm_b_gla_gate', 'm_gla_norm_w', 'm_s5_lam_re', 'm_s5_lam_im', 'm_s5_log_dt', 'm_s5_b_re', 'm_s5_b_im', 'm_s5_c_re', 'm_s5_c_im', 'm_s5_d', 'm_w_s5_glu', 'm_q_norm_w', 'm_k_norm_w', 'm_w_proj_gla', 'm_w_proj_s5', 'm_w_proj_attn', 'm_w_out', 'm_ln1_w', 'm_ln1_b', 'm_ln2_w', 'm_ln2_b', 'm_w_ffn_in', 'm_w_ffn_out', 'v_c_ctx', 'v_w_ada', 'v_b_ada', 'v_w_in', 'v_w_gla_gate', 'v_b_gla_gate', 'v_gla_norm_w', 'v_s5_lam_re', 'v_s5_lam_im', 'v_s5_log_dt', 'v_s5_b_re', 'v_s5_b_im', 'v_s5_c_re', 'v_s5_c_im', 'v_s5_d', 'v_w_s5_glu', 'v_q_norm_w', 'v_k_norm_w', 'v_w_proj_gla', 'v_w_proj_s5', 'v_w_proj_attn', 'v_w_out', 'v_ln1_w', 'v_ln1_b', 'v_ln2_w', 'v_ln2_b', 'v_w_ffn_in', 'v_w_ffn_out']
TWIN_OUTPUTS = ['loss', 'grad_x', 'grad_c_ctx', 'grad_w_ada', 'grad_b_ada', 'grad_w_in', 'grad_w_gla_gate', 'grad_b_gla_gate', 'grad_gla_norm_w', 'grad_s5_lam_re', 'grad_s5_lam_im', 'grad_s5_log_dt', 'grad_s5_b_re', 'grad_s5_b_im', 'grad_s5_c_re', 'grad_s5_c_im', 'grad_s5_d', 'grad_w_s5_glu', 'grad_q_norm_w', 'grad_k_norm_w', 'grad_w_proj_gla', 'grad_w_proj_s5', 'grad_w_proj_attn', 'grad_w_out', 'grad_ln1_w', 'grad_ln1_b', 'grad_ln2_w', 'grad_ln2_b', 'grad_w_ffn_in', 'grad_w_ffn_out', 'delta_c_ctx', 'delta_w_ada', 'delta_b_ada', 'delta_w_in', 'delta_w_gla_gate', 'delta_b_gla_gate', 'delta_gla_norm_w', 'delta_s5_lam_re', 'delta_s5_lam_im', 'delta_s5_log_dt', 'delta_s5_b_re', 'delta_s5_b_im', 'delta_s5_c_re', 'delta_s5_c_im', 'delta_s5_d', 'delta_w_s5_glu', 'delta_q_norm_w', 'delta_k_norm_w', 'delta_w_proj_gla', 'delta_w_proj_s5', 'delta_w_proj_attn', 'delta_w_out', 'delta_ln1_w', 'delta_ln1_b', 'delta_ln2_w', 'delta_ln2_b', 'delta_w_ffn_in', 'delta_w_ffn_out', 'new_m_c_ctx', 'new_m_w_ada', 'new_m_b_ada', 'new_m_w_in', 'new_m_w_gla_gate', 'new_m_b_gla_gate', 'new_m_gla_norm_w', 'new_m_s5_lam_re', 'new_m_s5_lam_im', 'new_m_s5_log_dt', 'new_m_s5_b_re', 'new_m_s5_b_im', 'new_m_s5_c_re', 'new_m_s5_c_im', 'new_m_s5_d', 'new_m_w_s5_glu', 'new_m_q_norm_w', 'new_m_k_norm_w', 'new_m_w_proj_gla', 'new_m_w_proj_s5', 'new_m_w_proj_attn', 'new_m_w_out', 'new_m_ln1_w', 'new_m_ln1_b', 'new_m_ln2_w', 'new_m_ln2_b', 'new_m_w_ffn_in', 'new_m_w_ffn_out', 'new_v_c_ctx', 'new_v_w_ada', 'new_v_b_ada', 'new_v_w_in', 'new_v_w_gla_gate', 'new_v_b_gla_gate', 'new_v_gla_norm_w', 'new_v_s5_lam_re', 'new_v_s5_lam_im', 'new_v_s5_log_dt', 'new_v_s5_b_re', 'new_v_s5_b_im', 'new_v_s5_c_re', 'new_v_s5_c_im', 'new_v_s5_d', 'new_v_w_s5_glu', 'new_v_q_norm_w', 'new_v_k_norm_w', 'new_v_w_proj_gla', 'new_v_w_proj_s5', 'new_v_w_proj_attn', 'new_v_w_out', 'new_v_ln1_w', 'new_v_ln1_b', 'new_v_ln2_w', 'new_v_ln2_b', 'new_v_w_ffn_in', 'new_v_w_ffn_out']
TWIN_LEAF_KINDS = {'loss': 'loss', 'grad_x': 'grad_x', 'grad_c_ctx': 'grad_w', 'grad_w_ada': 'grad_w', 'grad_b_ada': 'grad_w', 'grad_w_in': 'grad_w', 'grad_w_gla_gate': 'grad_w', 'grad_b_gla_gate': 'grad_w', 'grad_gla_norm_w': 'grad_w', 'grad_s5_lam_re': 'grad_w', 'grad_s5_lam_im': 'grad_w', 'grad_s5_log_dt': 'grad_w', 'grad_s5_b_re': 'grad_w', 'grad_s5_b_im': 'grad_w', 'grad_s5_c_re': 'grad_w', 'grad_s5_c_im': 'grad_w', 'grad_s5_d': 'grad_w', 'grad_w_s5_glu': 'grad_w', 'grad_q_norm_w': 'grad_w', 'grad_k_norm_w': 'grad_w', 'grad_w_proj_gla': 'grad_w', 'grad_w_proj_s5': 'grad_w', 'grad_w_proj_attn': 'grad_w', 'grad_w_out': 'grad_w', 'grad_ln1_w': 'grad_w', 'grad_ln1_b': 'grad_w', 'grad_ln2_w': 'grad_w', 'grad_ln2_b': 'grad_w', 'grad_w_ffn_in': 'grad_w', 'grad_w_ffn_out': 'grad_w', 'delta_c_ctx': 'delta_w', 'delta_w_ada': 'delta_w', 'delta_b_ada': 'delta_w', 'delta_w_in': 'delta_w', 'delta_w_gla_gate': 'delta_w', 'delta_b_gla_gate': 'delta_w', 'delta_gla_norm_w': 'delta_w', 'delta_s5_lam_re': 'delta_w', 'delta_s5_lam_im': 'delta_w', 'delta_s5_log_dt': 'delta_w', 'delta_s5_b_re': 'delta_w', 'delta_s5_b_im': 'delta_w', 'delta_s5_c_re': 'delta_w', 'delta_s5_c_im': 'delta_w', 'delta_s5_d': 'delta_w', 'delta_w_s5_glu': 'delta_w', 'delta_q_norm_w': 'delta_w', 'delta_k_norm_w': 'delta_w', 'delta_w_proj_gla': 'delta_w', 'delta_w_proj_s5': 'delta_w', 'delta_w_proj_attn': 'delta_w', 'delta_w_out': 'delta_w', 'delta_ln1_w': 'delta_w', 'delta_ln1_b': 'delta_w', 'delta_ln2_w': 'delta_w', 'delta_ln2_b': 'delta_w', 'delta_w_ffn_in': 'delta_w', 'delta_w_ffn_out': 'delta_w', 'new_m_c_ctx': 'new_m', 'new_m_w_ada': 'new_m', 'new_m_b_ada': 'new_m', 'new_m_w_in': 'new_m', 'new_m_w_gla_gate': 'new_m', 'new_m_b_gla_gate': 'new_m', 'new_m_gla_norm_w': 'new_m', 'new_m_s5_lam_re': 'new_m', 'new_m_s5_lam_im': 'new_m', 'new_m_s5_log_dt': 'new_m', 'new_m_s5_b_re': 'new_m', 'new_m_s5_b_im': 'new_m', 'new_m_s5_c_re': 'new_m', 'new_m_s5_c_im': 'new_m', 'new_m_s5_d': 'new_m', 'new_m_w_s5_glu': 'new_m', 'new_m_q_norm_w': 'new_m', 'new_m_k_norm_w': 'new_m', 'new_m_w_proj_gla': 'new_m', 'new_m_w_proj_s5': 'new_m', 'new_m_w_proj_attn': 'new_m', 'new_m_w_out': 'new_m', 'new_m_ln1_w': 'new_m', 'new_m_ln1_b': 'new_m', 'new_m_ln2_w': 'new_m', 'new_m_ln2_b': 'new_m', 'new_m_w_ffn_in': 'new_m', 'new_m_w_ffn_out': 'new_m', 'new_v_c_ctx': 'new_v', 'new_v_w_ada': 'new_v', 'new_v_b_ada': 'new_v', 'new_v_w_in': 'new_v', 'new_v_w_gla_gate': 'new_v', 'new_v_b_gla_gate': 'new_v', 'new_v_gla_norm_w': 'new_v', 'new_v_s5_lam_re': 'new_v', 'new_v_s5_lam_im': 'new_v', 'new_v_s5_log_dt': 'new_v', 'new_v_s5_b_re': 'new_v', 'new_v_s5_b_im': 'new_v', 'new_v_s5_c_re': 'new_v', 'new_v_s5_c_im': 'new_v', 'new_v_s5_d': 'new_v', 'new_v_w_s5_glu': 'new_v', 'new_v_q_norm_w': 'new_v', 'new_v_k_norm_w': 'new_v', 'new_v_w_proj_gla': 'new_v', 'new_v_w_proj_s5': 'new_v', 'new_v_w_proj_attn': 'new_v', 'new_v_w_out': 'new_v', 'new_v_ln1_w': 'new_v', 'new_v_ln1_b': 'new_v', 'new_v_ln2_w': 'new_v', 'new_v_ln2_b': 'new_v', 'new_v_w_ffn_in': 'new_v', 'new_v_w_ffn_out': 'new_v'}


def _forward(args):
    return _fwd_reference(*[args[k] for k in FWD_PARAMS])


def _output_shape():
    out = _jax.eval_shape(lambda: _forward(_fwd_setup_inputs(0)))
    return out.shape, out.dtype

N_MICROBATCH = 1
ADAM_LR = 0.001
ADAM_B1 = 0.9
ADAM_B2 = 0.999
ADAM_EPS = 1e-08
ADAM_WD = 0.01
ADAM_STEP = 10
PER_EXAMPLE_BATCH_AXIS = {'x': 0, 'c': 0, 'ctx': 0, 'loss_target': 0}
SHARED_INPUTS = []
_WEIGHT_DTYPES = {'c_ctx': _jnp.float32, 'w_ada': _jnp.float32, 'b_ada': _jnp.float32, 'w_in': _jnp.float32, 'w_gla_gate': _jnp.float32, 'b_gla_gate': _jnp.float32, 'gla_norm_w': _jnp.float32, 's5_lam_re': _jnp.float32, 's5_lam_im': _jnp.float32, 's5_log_dt': _jnp.float32, 's5_b_re': _jnp.float32, 's5_b_im': _jnp.float32, 's5_c_re': _jnp.float32, 's5_c_im': _jnp.float32, 's5_d': _jnp.float32, 'w_s5_glu': _jnp.float32, 'q_norm_w': _jnp.float32, 'k_norm_w': _jnp.float32, 'w_proj_gla': _jnp.float32, 'w_proj_s5': _jnp.float32, 'w_proj_attn': _jnp.float32, 'w_out': _jnp.float32, 'ln1_w': _jnp.float32, 'ln1_b': _jnp.float32, 'ln2_w': _jnp.float32, 'ln2_b': _jnp.float32, 'w_ffn_in': _jnp.float32, 'w_ffn_out': _jnp.float32}
MOMENT_SCALE = {'c_ctx': 1.023397e-03, 'w_ada': 4.033949e-03, 'b_ada': 6.892700e-03, 'w_in': 1.665448e-03, 'w_gla_gate': 4.141150e-04, 'b_gla_gate': 1.170404e-03, 'gla_norm_w': 2.371090e-03, 's5_lam_re': 6.659025e-04, 's5_lam_im': 6.484749e-04, 's5_log_dt': 2.940333e-01, 's5_b_re': 6.012183e-04, 's5_b_im': 6.066840e-04, 's5_c_re': 1.442637e-04, 's5_c_im': 1.471454e-04, 's5_d': 1.899282e-03, 'w_s5_glu': 7.399691e-04, 'q_norm_w': 8.221920e-04, 'k_norm_w': 8.197068e-04, 'w_proj_gla': 1.671721e-03, 'w_proj_s5': 1.309681e-03, 'w_proj_attn': 8.838980e-04, 'w_out': 5.433285e-03, 'ln1_w': 2.831670e-01, 'ln1_b': 1.433284e-01, 'ln2_w': 4.027971e+00, 'ln2_b': 2.454164e-01, 'w_ffn_in': 2.039273e-03, 'w_ffn_out': 7.931145e-03}


def _to_microbatches(a, axis):
    t = _jnp.moveaxis(a, axis, 0)
    t = t.reshape((N_MICROBATCH, t.shape[0] // N_MICROBATCH) + t.shape[1:])
    return _jnp.moveaxis(t, 1, axis + 1)


def setup_inputs(seed: int = 0) -> dict:
    inp = _fwd_setup_inputs(seed)
    key = _jax.random.fold_in(_jax.random.key(seed), 7919)
    shape, _ = _output_shape()
    out = dict(inp)
    out["loss_target"] = _jax.random.normal(_jax.random.fold_in(key, 0), shape, _jnp.float32)
    for i, name in enumerate(TWIN_WEIGHTS):
        w = inp[name].astype(_jnp.float32)
        if MOMENT_SCALE is None:
            s = _jnp.sqrt(_jnp.mean(_jnp.square(w)) + 1e-30)
        else:
            s = MOMENT_SCALE[name]
        km, kv = _jax.random.split(_jax.random.fold_in(key, i + 1))
        out[name] = w
        out["m_" + name] = s * _jax.random.normal(km, w.shape, _jnp.float32)
        out["v_" + name] = (s * s) * _jax.random.uniform(kv, w.shape, _jnp.float32, 0.5, 1.5)
    if N_MICROBATCH > 1:
        for name, axis in PER_EXAMPLE_BATCH_AXIS.items():
            out[name] = _to_microbatches(out[name], axis)
    return {'x': out['x'], 'c': out['c'], 'ctx': out['ctx'], 'c_ctx': out['c_ctx'], 'w_ada': out['w_ada'], 'b_ada': out['b_ada'], 'w_in': out['w_in'], 'w_gla_gate': out['w_gla_gate'], 'b_gla_gate': out['b_gla_gate'], 'gla_norm_w': out['gla_norm_w'], 's5_lam_re': out['s5_lam_re'], 's5_lam_im': out['s5_lam_im'], 's5_log_dt': out['s5_log_dt'], 's5_b_re': out['s5_b_re'], 's5_b_im': out['s5_b_im'], 's5_c_re': out['s5_c_re'], 's5_c_im': out['s5_c_im'], 's5_d': out['s5_d'], 'w_s5_glu': out['w_s5_glu'], 'q_norm_w': out['q_norm_w'], 'k_norm_w': out['k_norm_w'], 'w_proj_gla': out['w_proj_gla'], 'w_proj_s5': out['w_proj_s5'], 'w_proj_attn': out['w_proj_attn'], 'w_out': out['w_out'], 'ln1_w': out['ln1_w'], 'ln1_b': out['ln1_b'], 'ln2_w': out['ln2_w'], 'ln2_b': out['ln2_b'], 'w_ffn_in': out['w_ffn_in'], 'w_ffn_out': out['w_ffn_out'], 'loss_target': out['loss_target'], 'm_c_ctx': out['m_c_ctx'], 'm_w_ada': out['m_w_ada'], 'm_b_ada': out['m_b_ada'], 'm_w_in': out['m_w_in'], 'm_w_gla_gate': out['m_w_gla_gate'], 'm_b_gla_gate': out['m_b_gla_gate'], 'm_gla_norm_w': out['m_gla_norm_w'], 'm_s5_lam_re': out['m_s5_lam_re'], 'm_s5_lam_im': out['m_s5_lam_im'], 'm_s5_log_dt': out['m_s5_log_dt'], 'm_s5_b_re': out['m_s5_b_re'], 'm_s5_b_im': out['m_s5_b_im'], 'm_s5_c_re': out['m_s5_c_re'], 'm_s5_c_im': out['m_s5_c_im'], 'm_s5_d': out['m_s5_d'], 'm_w_s5_glu': out['m_w_s5_glu'], 'm_q_norm_w': out['m_q_norm_w'], 'm_k_norm_w': out['m_k_norm_w'], 'm_w_proj_gla': out['m_w_proj_gla'], 'm_w_proj_s5': out['m_w_proj_s5'], 'm_w_proj_attn': out['m_w_proj_attn'], 'm_w_out': out['m_w_out'], 'm_ln1_w': out['m_ln1_w'], 'm_ln1_b': out['m_ln1_b'], 'm_ln2_w': out['m_ln2_w'], 'm_ln2_b': out['m_ln2_b'], 'm_w_ffn_in': out['m_w_ffn_in'], 'm_w_ffn_out': out['m_w_ffn_out'], 'v_c_ctx': out['v_c_ctx'], 'v_w_ada': out['v_w_ada'], 'v_b_ada': out['v_b_ada'], 'v_w_in': out['v_w_in'], 'v_w_gla_gate': out['v_w_gla_gate'], 'v_b_gla_gate': out['v_b_gla_gate'], 'v_gla_norm_w': out['v_gla_norm_w'], 'v_s5_lam_re': out['v_s5_lam_re'], 'v_s5_lam_im': out['v_s5_lam_im'], 'v_s5_log_dt': out['v_s5_log_dt'], 'v_s5_b_re': out['v_s5_b_re'], 'v_s5_b_im': out['v_s5_b_im'], 'v_s5_c_re': out['v_s5_c_re'], 'v_s5_c_im': out['v_s5_c_im'], 'v_s5_d': out['v_s5_d'], 'v_w_s5_glu': out['v_w_s5_glu'], 'v_q_norm_w': out['v_q_norm_w'], 'v_k_norm_w': out['v_k_norm_w'], 'v_w_proj_gla': out['v_w_proj_gla'], 'v_w_proj_s5': out['v_w_proj_s5'], 'v_w_proj_attn': out['v_w_proj_attn'], 'v_w_out': out['v_w_out'], 'v_ln1_w': out['v_ln1_w'], 'v_ln1_b': out['v_ln1_b'], 'v_ln2_w': out['v_ln2_w'], 'v_ln2_b': out['v_ln2_b'], 'v_w_ffn_in': out['v_w_ffn_in'], 'v_w_ffn_out': out['v_w_ffn_out']}


def _loss(weights, diff, rest, loss_target):
    with _jax.named_scope("forward"):
        args = {**rest, TWIN_DIFF_INPUT: diff, **{k: w.astype(_WEIGHT_DTYPES[k]) for k, w in weights.items()}}
        y = _forward(args)
    with _jax.named_scope("loss_head"):
        err = _jnp.square(y.astype(_jnp.float32) - loss_target)
        return 0.5 * _jnp.sum(_jnp.mean(err, axis=-1)) if err.ndim else 0.5 * err


def _adamw(w, g, m, v):
    m = ADAM_B1 * m + (1.0 - ADAM_B1) * g
    v = ADAM_B2 * v + (1.0 - ADAM_B2) * _jnp.square(g)
    m_hat = m / (1.0 - ADAM_B1 ** ADAM_STEP)
    v_hat = v / (1.0 - ADAM_B2 ** ADAM_STEP)
    delta = -ADAM_LR * (m_hat / (_jnp.sqrt(v_hat) + ADAM_EPS) + ADAM_WD * w)
    return delta, m, v


def reference(x, c, ctx, c_ctx, w_ada, b_ada, w_in, w_gla_gate, b_gla_gate, gla_norm_w, s5_lam_re, s5_lam_im, s5_log_dt, s5_b_re, s5_b_im, s5_c_re, s5_c_im, s5_d, w_s5_glu, q_norm_w, k_norm_w, w_proj_gla, w_proj_s5, w_proj_attn, w_out, ln1_w, ln1_b, ln2_w, ln2_b, w_ffn_in, w_ffn_out, loss_target, m_c_ctx, m_w_ada, m_b_ada, m_w_in, m_w_gla_gate, m_b_gla_gate, m_gla_norm_w, m_s5_lam_re, m_s5_lam_im, m_s5_log_dt, m_s5_b_re, m_s5_b_im, m_s5_c_re, m_s5_c_im, m_s5_d, m_w_s5_glu, m_q_norm_w, m_k_norm_w, m_w_proj_gla, m_w_proj_s5, m_w_proj_attn, m_w_out, m_ln1_w, m_ln1_b, m_ln2_w, m_ln2_b, m_w_ffn_in, m_w_ffn_out, v_c_ctx, v_w_ada, v_b_ada, v_w_in, v_w_gla_gate, v_b_gla_gate, v_gla_norm_w, v_s5_lam_re, v_s5_lam_im, v_s5_log_dt, v_s5_b_re, v_s5_b_im, v_s5_c_re, v_s5_c_im, v_s5_d, v_w_s5_glu, v_q_norm_w, v_k_norm_w, v_w_proj_gla, v_w_proj_s5, v_w_proj_attn, v_w_out, v_ln1_w, v_ln1_b, v_ln2_w, v_ln2_b, v_w_ffn_in, v_w_ffn_out):
    given = dict(x=x, c=c, ctx=ctx, c_ctx=c_ctx, w_ada=w_ada, b_ada=b_ada, w_in=w_in, w_gla_gate=w_gla_gate, b_gla_gate=b_gla_gate, gla_norm_w=gla_norm_w, s5_lam_re=s5_lam_re, s5_lam_im=s5_lam_im, s5_log_dt=s5_log_dt, s5_b_re=s5_b_re, s5_b_im=s5_b_im, s5_c_re=s5_c_re, s5_c_im=s5_c_im, s5_d=s5_d, w_s5_glu=w_s5_glu, q_norm_w=q_norm_w, k_norm_w=k_norm_w, w_proj_gla=w_proj_gla, w_proj_s5=w_proj_s5, w_proj_attn=w_proj_attn, w_out=w_out, ln1_w=ln1_w, ln1_b=ln1_b, ln2_w=ln2_w, ln2_b=ln2_b, w_ffn_in=w_ffn_in, w_ffn_out=w_ffn_out, loss_target=loss_target, m_c_ctx=m_c_ctx, m_w_ada=m_w_ada, m_b_ada=m_b_ada, m_w_in=m_w_in, m_w_gla_gate=m_w_gla_gate, m_b_gla_gate=m_b_gla_gate, m_gla_norm_w=m_gla_norm_w, m_s5_lam_re=m_s5_lam_re, m_s5_lam_im=m_s5_lam_im, m_s5_log_dt=m_s5_log_dt, m_s5_b_re=m_s5_b_re, m_s5_b_im=m_s5_b_im, m_s5_c_re=m_s5_c_re, m_s5_c_im=m_s5_c_im, m_s5_d=m_s5_d, m_w_s5_glu=m_w_s5_glu, m_q_norm_w=m_q_norm_w, m_k_norm_w=m_k_norm_w, m_w_proj_gla=m_w_proj_gla, m_w_proj_s5=m_w_proj_s5, m_w_proj_attn=m_w_proj_attn, m_w_out=m_w_out, m_ln1_w=m_ln1_w, m_ln1_b=m_ln1_b, m_ln2_w=m_ln2_w, m_ln2_b=m_ln2_b, m_w_ffn_in=m_w_ffn_in, m_w_ffn_out=m_w_ffn_out, v_c_ctx=v_c_ctx, v_w_ada=v_w_ada, v_b_ada=v_b_ada, v_w_in=v_w_in, v_w_gla_gate=v_w_gla_gate, v_b_gla_gate=v_b_gla_gate, v_gla_norm_w=v_gla_norm_w, v_s5_lam_re=v_s5_lam_re, v_s5_lam_im=v_s5_lam_im, v_s5_log_dt=v_s5_log_dt, v_s5_b_re=v_s5_b_re, v_s5_b_im=v_s5_b_im, v_s5_c_re=v_s5_c_re, v_s5_c_im=v_s5_c_im, v_s5_d=v_s5_d, v_w_s5_glu=v_w_s5_glu, v_q_norm_w=v_q_norm_w, v_k_norm_w=v_k_norm_w, v_w_proj_gla=v_w_proj_gla, v_w_proj_s5=v_w_proj_s5, v_w_proj_attn=v_w_proj_attn, v_w_out=v_w_out, v_ln1_w=v_ln1_w, v_ln1_b=v_ln1_b, v_ln2_w=v_ln2_w, v_ln2_b=v_ln2_b, v_w_ffn_in=v_w_ffn_in, v_w_ffn_out=v_w_ffn_out)
    weights = {n: given[n] for n in TWIN_WEIGHTS}
    shared = {n: given[n] for n in SHARED_INPUTS}
    per_example = {n: given[n] for n in ['x', 'c', 'ctx']}
    grad_fn = _jax.value_and_grad(_loss, argnums=(0, 1))

    def one_microbatch(ex, loss_target):
        ex = dict(ex)
        diff = ex.pop(TWIN_DIFF_INPUT)
        return grad_fn(weights, diff, {**shared, **ex}, loss_target)

    if N_MICROBATCH == 1:
        loss, (grad_w, grad_x) = one_microbatch(per_example, given["loss_target"])
    else:
        def body(carry, xs):
            loss_sum, grad_sum = carry
            l_k, (gw_k, gx_k) = one_microbatch(xs[0], xs[1])
            with _jax.named_scope("update"):
                return (loss_sum + l_k, _jax.tree.map(_jnp.add, grad_sum, gw_k)), gx_k

        init = (_jnp.zeros((), _jnp.float32), _jax.tree.map(_jnp.zeros_like, weights))
        (loss, grad_w), grad_x = _jax.lax.scan(body, init, (per_example, given["loss_target"]))
    with _jax.named_scope("update"):
        delta_w, new_m, new_v = {}, {}, {}
        for n in TWIN_WEIGHTS:
            delta_w[n], new_m[n], new_v[n] = _adamw(weights[n], grad_w[n], given["m_" + n], given["v_" + n])
    return (loss, grad_x, *[grad_w[n] for n in TWIN_WEIGHTS], *[delta_w[n] for n in TWIN_WEIGHTS],
            *[new_m[n] for n in TWIN_WEIGHTS], *[new_v[n] for n in TWIN_WEIGHTS])
```

```python
import functools
import math

import numpy as np
import jax
import jax.numpy as jnp
from jax import lax
from jax.experimental import pallas as pl
from jax.experimental.pallas import tpu as pltpu

F32 = jnp.float32
BF16 = jnp.bfloat16
MESH = pl.DeviceIdType.MESH
N_DEV = 8
VMEM_LIMIT_V7X = 48 * 1024 * 1024
LANE = 128

GRID_W = 64
GLA_HEADS, GLA_DK, GLA_DV = 4, 128, 256
GLA_QK, GLA_V = GLA_HEADS * GLA_DK, GLA_HEADS * GLA_DV
GLA_RANK, GLA_TAU, GLA_CHUNK = 16, 16.0, 64
S5_WIDTH, S5_GROUP, S5_GROUPS, S5_STATE = 768, 16, 48, 64
S5_CH = S5_GROUPS * S5_STATE
S5_ROWS = S5_CH // LANE
ATT_QH, ATT_KVH, ATT_D = 8, 2, 128
ATT_Q, ATT_KV = ATT_QH * ATT_D, ATT_KVH * ATT_D
ROPE_THETA = 10000.0
EPS = 1e-6
ADAM_LR, ADAM_B1, ADAM_B2, ADAM_EPS, ADAM_WD, ADAM_STEP = 0.001, 0.9, 0.999, 1e-08, 0.01, 10

WEIGHTS = ['c_ctx', 'w_ada', 'b_ada', 'w_in', 'w_gla_gate', 'b_gla_gate', 'gla_norm_w', 's5_lam_re',
           's5_lam_im', 's5_log_dt', 's5_b_re', 's5_b_im', 's5_c_re', 's5_c_im', 's5_d', 'w_s5_glu',
           'q_norm_w', 'k_norm_w', 'w_proj_gla', 'w_proj_s5', 'w_proj_attn', 'w_out', 'ln1_w', 'ln1_b',
           'ln2_w', 'ln2_b', 'w_ffn_in', 'w_ffn_out']
GATHERED = ['w_in', 'w_gla_gate', 'b_gla_gate', 'w_s5_glu', 'w_proj_gla', 'w_proj_s5', 'w_proj_attn',
            'w_out', 'w_ffn_in', 'w_ffn_out']
REPLICATED = [n for n in WEIGHTS if n not in GATHERED and n != 'w_ada']


def _params(sem=None):
    return pltpu.CompilerParams(dimension_semantics=sem, vmem_limit_bytes=VMEM_LIMIT_V7X)


def _pick(dim, target, align):
    best = None
    t = align
    while t <= min(dim, target):
        if dim % t == 0:
            best = t
        t += align
    return dim if best is None else best


_CONTRACT = {'nn': ((1,), (0,)), 'nt': ((1,), (1,)), 'tn': ((0,), (0,))}


def _dot(a, b, dims):
    return lax.dot_general(a.astype(BF16), b.astype(BF16), (_CONTRACT[dims], ((), ())),
                           preferred_element_type=F32)


@functools.partial(jax.custom_vjp, nondiff_argnums=(2,))
def _mm(a, b, dims):
    return _dot(a, b, dims)


def _mm_fwd(a, b, dims):
    return _dot(a, b, dims), (a, b)


def _mm_bwd(dims, res, g):
    a, b = res
    if dims == 'nn':
        return _dot(g, b, 'nt'), _dot(a, g, 'tn')
    if dims == 'nt':
        return _dot(g, b, 'nn'), _dot(g, a, 'tn')
    return _dot(b, g, 'nt'), _dot(a, g, 'nn')


_mm.defvjp(_mm_fwd, _mm_bwd)


@jax.custom_vjp
def _tri_dot(tri, tri_t, z):
    return lax.dot_general(tri, z, (((1,), (0,)), ((), ())), precision=lax.Precision.HIGHEST,
                           preferred_element_type=F32)


def _tri_dot_fwd(tri, tri_t, z):
    return _tri_dot(tri, tri_t, z), (tri, tri_t)


def _tri_dot_bwd(res, g):
    tri, tri_t = res
    dz = lax.dot_general(tri_t, g, (((1,), (0,)), ((), ())), precision=lax.Precision.HIGHEST,
                         preferred_element_type=F32)
    return jnp.zeros_like(tri), jnp.zeros_like(tri_t), dz


_tri_dot.defvjp(_tri_dot_fwd, _tri_dot_bwd)


def _tri_dot_plain(tri, tri_t, z):
    return lax.dot_general(tri, z, (((1,), (0,)), ((), ())), precision=lax.Precision.HIGHEST,
                           preferred_element_type=F32)


@jax.custom_vjp
def _swap_halves(x):
    return pltpu.roll(x, LANE // 2, 1)


def _swap_fwd(x):
    return pltpu.roll(x, LANE // 2, 1), None


def _swap_bwd(_, g):
    return (pltpu.roll(g, LANE // 2, 1),)


_swap_halves.defvjp(_swap_fwd, _swap_bwd)


def _swap_plain(x):
    return pltpu.roll(x, LANE // 2, 1)


def pmatmul(a, b, dims, name):
    if dims == 'nn':
        (m, k), (k2, n) = a.shape, b.shape
    elif dims == 'nt':
        (m, k), (n, k2) = a.shape, b.shape
    else:
        (k, m), (k2, n) = a.shape, b.shape
    assert k == k2, (a.shape, b.shape, dims)
    tm = _pick(m, 768, LANE if dims == 'tn' else 8)
    tn = _pick(n, 1024, LANE)
    tk = _pick(k, 1024, LANE)
    nk = k // tk
    a_spec = pl.BlockSpec((tk, tm), lambda i, j, kk: (kk, i)) if dims == 'tn' else \
        pl.BlockSpec((tm, tk), lambda i, j, kk: (i, kk))
    b_spec = pl.BlockSpec((tn, tk), lambda i, j, kk: (j, kk)) if dims == 'nt' else \
        pl.BlockSpec((tk, tn), lambda i, j, kk: (kk, j))

    def body(a_ref, b_ref, o_ref, acc_ref):
        kk = pl.program_id(2)

        @pl.when(kk == 0)
        def _():
            acc_ref[...] = jnp.zeros_like(acc_ref)

        acc_ref[...] += _dot(a_ref[...], b_ref[...], dims)

        @pl.when(kk == nk - 1)
        def _():
            o_ref[...] = acc_ref[...]

    return pl.pallas_call(
        body, name=name, grid=(m // tm, n // tn, nk),
        in_specs=[a_spec, b_spec], out_specs=pl.BlockSpec((tm, tn), lambda i, j, kk: (i, j)),
        out_shape=jax.ShapeDtypeStruct((m, n), F32),
        scratch_shapes=[pltpu.VMEM((tm, tn), F32)],
        compiler_params=_params(("parallel", "parallel", "arbitrary")),
    )(a, b)


def make_linear(name):
    @jax.custom_vjp
    def linear(a, w, w_proxy):
        return pmatmul(a, w, 'nn', name + '_fwd')

    def fwd(a, w, w_proxy):
        return pmatmul(a, w, 'nn', name + '_fwd'), (a, w)

    def bwd(res, g):
        a, w = res
        return pmatmul(g, w, 'nt', name + '_da'), jnp.zeros_like(w), pmatmul(a, g, 'tn', name + '_dw')

    linear.defvjp(fwd, bwd)
    return linear


def make_rowwise(name, f, out_widths, tile, n_nondiff=0, f_diff=None):
    f_diff = f if f_diff is None else f_diff

    def fwd_call(rows, params):
        n = rows[0].shape[0]
        nr, npar = len(rows), len(params)

        def body(*refs):
            i = pl.program_id(0)
            outs = f(i, tuple(r[...] for r in refs[:nr]), tuple(p[...] for p in refs[nr:nr + npar]))
            for o_ref, val in zip(refs[nr + npar:], outs):
                o_ref[...] = val

        return pl.pallas_call(
            body, name=name + '_fwd', grid=(n // tile,),
            in_specs=[pl.BlockSpec((tile, r.shape[1]), lambda i: (i, 0)) for r in rows] +
                     [pl.BlockSpec(p.shape, lambda i: (0, 0)) for p in params],
            out_specs=[pl.BlockSpec((tile, w), lambda i: (i, 0)) for w in out_widths],
            out_shape=[jax.ShapeDtypeStruct((n, w), F32) for w in out_widths],
            compiler_params=_params(("parallel",)),
        )(*rows, *params)

    def bwd_call(rows, params, cots):
        n = rows[0].shape[0]
        nr, npar, nc = len(rows), len(params), len(cots)
        nd = nr - n_nondiff

        def body(*refs):
            i = pl.program_id(0)
            rv = tuple(r[...] for r in refs[:nr])
            pv = tuple(p[...] for p in refs[nr:nr + npar])
            cv = tuple(c[...] for c in refs[nr + npar:nr + npar + nc])
            out_refs = refs[nr + npar + nc:]
            _, vjp = jax.vjp(lambda dr, pp: tuple(f_diff(i, dr + rv[nd:], pp)), rv[:nd], pv)
            drows, dparams = vjp(cv)
            for o_ref, val in zip(out_refs[:nd], drows):
                o_ref[...] = val

            @pl.when(i == 0)
            def _():
                for o_ref in out_refs[nd:]:
                    o_ref[...] = jnp.zeros_like(o_ref)

            for o_ref, val in zip(out_refs[nd:], dparams):
                o_ref[...] += val

        outs = pl.pallas_call(
            body, name=name + '_bwd', grid=(n // tile,),
            in_specs=[pl.BlockSpec((tile, r.shape[1]), lambda i: (i, 0)) for r in rows] +
                     [pl.BlockSpec(p.shape, lambda i: (0, 0)) for p in params] +
                     [pl.BlockSpec((tile, c.shape[1]), lambda i: (i, 0)) for c in cots],
            out_specs=[pl.BlockSpec((tile, r.shape[1]), lambda i: (i, 0)) for r in rows[:nd]] +
                      [pl.BlockSpec(p.shape, lambda i: (0, 0)) for p in params],
            out_shape=[jax.ShapeDtypeStruct(r.shape, F32) for r in rows[:nd]] +
                      [jax.ShapeDtypeStruct(p.shape, F32) for p in params],
            compiler_params=_params(("arbitrary",)),
        )(*rows, *params, *cots)
        drows = tuple(outs[:nd]) + tuple(jnp.zeros_like(r) for r in rows[nd:])
        return drows, tuple(outs[nd:])

    @jax.custom_vjp
    def op(rows, params):
        return tuple(fwd_call(rows, params))

    def fwd(rows, params):
        return tuple(fwd_call(rows, params)), (rows, params)

    def bwd(res, cots):
        rows, params = res
        return bwd_call(rows, params, tuple(cots))

    op.defvjp(fwd, bwd)
    return op


def _silu(x):
    return x * jax.nn.sigmoid(x)


def _layer_norm(z, w, b):
    mu = jnp.mean(z, -1, keepdims=True)
    zc = z - mu
    var = jnp.mean(zc * zc, -1, keepdims=True)
    y = zc * lax.rsqrt(var + EPS)
    return y if w is None else y * w + b


def _softmax_rows(s):
    e = jnp.exp(s - jnp.max(s, -1, keepdims=True))
    return e / jnp.sum(e, -1, keepdims=True)


def attention_fwd_call(q, k, v, n_ctx):
    n = q.shape[0]
    tq = n_ctx
    grp = ATT_QH // ATT_KVH
    scale = ATT_D ** -0.5

    def body(q_ref, k_ref, v_ref, o_ref):
        def run(nk):
            p = _softmax_rows(_dot(q_ref[...], k_ref[0:nk, :], 'nt') * scale)
            o_ref[...] = _dot(p, v_ref[0:nk, :], 'nn')

        pl.when(pl.program_id(2) == 0)(lambda: run(n_ctx))
        pl.when(pl.program_id(2) > 0)(lambda: run(n))

    return pl.pallas_call(
        body, name='attn_fwd', grid=(ATT_KVH, grp, n // tq),
        in_specs=[pl.BlockSpec((tq, ATT_D), lambda h, g, i: (i, h * grp + g)),
                  pl.BlockSpec((n, ATT_D), lambda h, g, i: (0, h)),
                  pl.BlockSpec((n, ATT_D), lambda h, g, i: (0, h))],
        out_specs=pl.BlockSpec((tq, ATT_D), lambda h, g, i: (i, h * grp + g)),
        out_shape=jax.ShapeDtypeStruct((n, ATT_Q), F32),
        compiler_params=_params(("parallel", "parallel", "parallel")),
    )(q, k, v)


def attention_bwd_call(q, k, v, do, n_ctx):
    n = q.shape[0]
    tq = n_ctx
    grp = ATT_QH // ATT_KVH
    scale = ATT_D ** -0.5

    def body(q_ref, k_ref, v_ref, do_ref, dq_ref, dk_ref, dv_ref):
        @pl.when(jnp.logical_and(pl.program_id(1) == 0, pl.program_id(2) == 0))
        def _():
            dk_ref[...] = jnp.zeros_like(dk_ref)
            dv_ref[...] = jnp.zeros_like(dv_ref)

        def run(nk):
            qv, kv, vv, dov = q_ref[...], k_ref[0:nk, :], v_ref[0:nk, :], do_ref[...]
            p = _softmax_rows(_dot(qv, kv, 'nt') * scale)
            dv_ref[0:nk, :] += _dot(p, dov, 'tn')
            dp = _dot(dov, vv, 'nt')
            ds = p * (dp - jnp.sum(dp * p, -1, keepdims=True)) * scale
            dq_ref[...] = _dot(ds, kv, 'nn')
            dk_ref[0:nk, :] += _dot(ds, qv, 'tn')

        pl.when(pl.program_id(2) == 0)(lambda: run(n_ctx))
        pl.when(pl.program_id(2) > 0)(lambda: run(n))

    return pl.pallas_call(
        body, name='attn_bwd', grid=(ATT_KVH, grp, n // tq),
        in_specs=[pl.BlockSpec((tq, ATT_D), lambda h, g, i: (i, h * grp + g)),
                  pl.BlockSpec((n, ATT_D), lambda h, g, i: (0, h)),
                  pl.BlockSpec((n, ATT_D), lambda h, g, i: (0, h)),
                  pl.BlockSpec((tq, ATT_D), lambda h, g, i: (i, h * grp + g))],
        out_specs=[pl.BlockSpec((tq, ATT_D), lambda h, g, i: (i, h * grp + g)),
                   pl.BlockSpec((n, ATT_D), lambda h, g, i: (0, h)),
                   pl.BlockSpec((n, ATT_D), lambda h, g, i: (0, h))],
        out_shape=[jax.ShapeDtypeStruct((n, ATT_Q), F32), jax.ShapeDtypeStruct((n, ATT_KV), F32),
                   jax.ShapeDtypeStruct((n, ATT_KV), F32)],
        compiler_params=_params(("parallel", "arbitrary", "arbitrary")),
    )(q, k, v, do)


def make_attention(n_ctx):
    @jax.custom_vjp
    def attention(q, k, v):
        return attention_fwd_call(q, k, v, n_ctx)

    def fwd(q, k, v):
        return attention_fwd_call(q, k, v, n_ctx), (q, k, v)

    def bwd(res, do):
        return tuple(attention_bwd_call(*res, do, n_ctx))

    attention.defvjp(fwd, bwd)
    return attention


def _log_sigmoid(z):
    return jnp.minimum(z, 0.0) - jnp.log(1.0 + jnp.exp(-jnp.abs(z)))


def _gla_chunk(d, q, k, v, glr, wg, bg, state, mm, tri_dot):
    chunk = q.shape[0]
    r = lax.broadcasted_iota(jnp.int32, (chunk, chunk), 0)
    c = lax.broadcasted_iota(jnp.int32, (chunk, chunk), 1)
    lower = jnp.where(c <= r, 1.0, 0.0).astype(F32)
    upper = jnp.where(c >= r, 1.0, 0.0).astype(F32)
    tri = jnp.where(d == 0, lower, upper)
    tri_t = jnp.where(d == 0, upper, lower)
    log_a = _log_sigmoid(mm(glr, wg, 'nn') + bg) * (1.0 / GLA_TAU)
    outs, new_state = [], []
    for h in range(GLA_HEADS):
        la = log_a[:, h * GLA_DK:(h + 1) * GLA_DK]
        b = tri_dot(tri, tri_t, la)
        tot = jnp.sum(la, axis=0, keepdims=True)
        qh = q[:, h * GLA_DK:(h + 1) * GLA_DK] * (GLA_DK ** -0.5)
        kh = k[:, h * GLA_DK:(h + 1) * GLA_DK]
        vh = v[:, h * GLA_DV:(h + 1) * GLA_DV]
        qe = qh * jnp.exp(b)
        ke = kh * jnp.exp(-b)
        att = mm(qe, ke, 'nt') * tri
        outs.append(mm(att, vh, 'nn') + mm(qe, state[h], 'nn'))
        kd = kh * jnp.exp(tot - b)
        decay = jnp.transpose(jnp.broadcast_to(jnp.exp(tot), (GLA_DK, GLA_DK)))
        decay = jnp.concatenate([decay] * (GLA_DV // GLA_DK), axis=1)
        new_state.append(decay * state[h] + mm(kd, vh, 'tn'))
    return jnp.concatenate(outs, axis=1), tuple(new_state)


def _gla_chunk_of(d, s, nc, ns):
    falling = jnp.where(s < nc, nc - 1 - s, ns - 1 - (s - nc))
    return jnp.where(d == 0, s, falling)


def gla_fwd_call(q, k, v, glr, wg, bg, n_ctx):
    n = q.shape[0]
    ch = GLA_CHUNK
    ns, nc = n // ch, n_ctx // ch

    def at(d, s):
        return _gla_chunk_of(d, s, nc, ns)

    def body(q_ref, k_ref, v_ref, glr_ref, wg_ref, bg_ref, o_ref, saved_ref, state_ref):
        d, s = pl.program_id(0), pl.program_id(1)

        @pl.when(s == 0)
        def _():
            state_ref[...] = jnp.zeros_like(state_ref)

        saved_ref[...] = state_ref[...]
        state = tuple(state_ref[h] for h in range(GLA_HEADS))
        o, new_state = _gla_chunk(d, q_ref[...], k_ref[...], v_ref[...], glr_ref[...], wg_ref[...],
                                  bg_ref[...], state, _dot, _tri_dot_plain)
        o_ref[...] = o
        for h in range(GLA_HEADS):
            state_ref[h] = new_state[h]

    return pl.pallas_call(
        body, name='gla_fwd', grid=(2, ns),
        in_specs=[pl.BlockSpec((ch, GLA_QK), lambda d, s: (at(d, s), 0)),
                  pl.BlockSpec((ch, GLA_QK), lambda d, s: (at(d, s), 0)),
                  pl.BlockSpec((ch, GLA_V), lambda d, s: (at(d, s), 0)),
                  pl.BlockSpec((ch, LANE), lambda d, s: (at(d, s), 0)),
                  pl.BlockSpec((None, LANE, GLA_QK), lambda d, s: (d, 0, 0)),
                  pl.BlockSpec((None, 1, GLA_QK), lambda d, s: (d, 0, 0))],
        out_specs=[pl.BlockSpec((None, ch, GLA_V), lambda d, s: (d, at(d, s), 0)),
                   pl.BlockSpec((None, None, GLA_HEADS, GLA_DK, GLA_DV), lambda d, s: (d, s, 0, 0, 0))],
        out_shape=[jax.ShapeDtypeStruct((2, n, GLA_V), F32),
                   jax.ShapeDtypeStruct((2, ns, GLA_HEADS, GLA_DK, GLA_DV), F32)],
        scratch_shapes=[pltpu.VMEM((GLA_HEADS, GLA_DK, GLA_DV), F32)],
        compiler_params=_params(("parallel", "arbitrary")),
    )(q, k, v, glr, wg, bg)


def gla_bwd_call(q, k, v, glr, wg, bg, saved, do, n_ctx):
    n = q.shape[0]
    ch = GLA_CHUNK
    ns, nc = n // ch, n_ctx // ch

    def at(d, s):
        return _gla_chunk_of(d, ns - 1 - s, nc, ns)

    def body(q_ref, k_ref, v_ref, glr_ref, wg_ref, bg_ref, saved_ref, do_ref,
             dq_ref, dk_ref, dv_ref, dglr_ref, dwg_ref, dbg_ref, dstate_ref):
        d, s = pl.program_id(0), pl.program_id(1)

        @pl.when(s == 0)
        def _():
            dstate_ref[...] = jnp.zeros_like(dstate_ref)
            dwg_ref[...] = jnp.zeros_like(dwg_ref)
            dbg_ref[...] = jnp.zeros_like(dbg_ref)

        state = tuple(saved_ref[h] for h in range(GLA_HEADS))
        dstate = tuple(dstate_ref[h] for h in range(GLA_HEADS))

        def f(qv, kv, vv, glrv, wgv, bgv, st):
            return _gla_chunk(d, qv, kv, vv, glrv, wgv, bgv, st, _mm, _tri_dot)

        _, vjp = jax.vjp(f, q_ref[...], k_ref[...], v_ref[...], glr_ref[...], wg_ref[...], bg_ref[...], state)
        dq, dk, dv, dglr, dwg, dbg, dprev = vjp((do_ref[...], dstate))
        dq_ref[...] = dq
        dk_ref[...] = dk
        dv_ref[...] = dv
        dglr_ref[...] = dglr
        dwg_ref[...] += dwg
        dbg_ref[...] += dbg
        for h in range(GLA_HEADS):
            dstate_ref[h] = dprev[h]

    return pl.pallas_call(
        body, name='gla_bwd', grid=(2, ns),
        in_specs=[pl.BlockSpec((ch, GLA_QK), lambda d, s: (at(d, s), 0)),
                  pl.BlockSpec((ch, GLA_QK), lambda d, s: (at(d, s), 0)),
                  pl.BlockSpec((ch, GLA_V), lambda d, s: (at(d, s), 0)),
                  pl.BlockSpec((ch, LANE), lambda d, s: (at(d, s), 0)),
                  pl.BlockSpec((None, LANE, GLA_QK), lambda d, s: (d, 0, 0)),
                  pl.BlockSpec((None, 1, GLA_QK), lambda d, s: (d, 0, 0)),
                  pl.BlockSpec((None, None, GLA_HEADS, GLA_DK, GLA_DV), lambda d, s: (d, ns - 1 - s, 0, 0, 0)),
                  pl.BlockSpec((None, ch, GLA_V), lambda d, s: (d, at(d, s), 0))],
        out_specs=[pl.BlockSpec((None, ch, GLA_QK), lambda d, s: (d, at(d, s), 0)),
                   pl.BlockSpec((None, ch, GLA_QK), lambda d, s: (d, at(d, s), 0)),
                   pl.BlockSpec((None, ch, GLA_V), lambda d, s: (d, at(d, s), 0)),
                   pl.BlockSpec((None, ch, LANE), lambda d, s: (d, at(d, s), 0)),
                   pl.BlockSpec((None, LANE, GLA_QK), lambda d, s: (d, 0, 0)),
                   pl.BlockSpec((None, 1, GLA_QK), lambda d, s: (d, 0, 0))],
        out_shape=[jax.ShapeDtypeStruct((2, n, GLA_QK), F32), jax.ShapeDtypeStruct((2, n, GLA_QK), F32),
                   jax.ShapeDtypeStruct((2, n, GLA_V), F32), jax.ShapeDtypeStruct((2, n, LANE), F32),
                   jax.ShapeDtypeStruct((2, LANE, GLA_QK), F32), jax.ShapeDtypeStruct((2, 1, GLA_QK), F32)],
        scratch_shapes=[pltpu.VMEM((GLA_HEADS, GLA_DK, GLA_DV), F32)],
        compiler_params=_params(("parallel", "arbitrary")),
    )(q, k, v, glr, wg, bg, saved, do)


def make_gla(n_ctx):
    @jax.custom_vjp
    def gla(q, k, v, glr, wg, bg):
        return gla_fwd_call(q, k, v, glr, wg, bg, n_ctx)[0]

    def fwd(q, k, v, glr, wg, bg):
        o, saved = gla_fwd_call(q, k, v, glr, wg, bg, n_ctx)
        return o, (q, k, v, glr, wg, bg, saved)

    def bwd(res, do):
        dq, dk, dv, dglr, dwg, dbg = gla_bwd_call(*res, do, n_ctx)
        return dq[0] + dq[1], dk[0] + dk[1], dv[0] + dv[1], dglr[0] + dglr[1], dwg, dbg

    gla.defvjp(fwd, bwd)
    return gla


def _s5_orders(n_chunks):
    d0 = [(j, True) for j in range(n_chunks)]
    d1 = [(0, False)] + [(j, False) for j in range(n_chunks - 1, 0, -1)]
    rev = lambda o: [(j, not up) for j, up in reversed(o)]
    return d0, d1, rev(d0), rev(d1)


def s5_scan_call(xr, xi, ar, ai, order, lc, name):
    t = xr.shape[0]
    assert t == lc * len(order)
    ids = jnp.asarray([j for j, _ in order], jnp.int32)
    ups = jnp.asarray([int(up) for _, up in order], jnp.int32)

    def body(ids_ref, ups_ref, xr_ref, xi_ref, ar_ref, ai_ref, sr_ref, si_ref, st_ref):
        g = pl.program_id(0)

        @pl.when(g == 0)
        def _():
            st_ref[...] = jnp.zeros_like(st_ref)

        a_r, a_i = ar_ref[...], ai_ref[...]
        up = ups_ref[g]

        def step(j, carry):
            cr, ci = carry
            tt = jnp.where(up == 1, j, lc - 1 - j)
            nr = a_r * cr - a_i * ci + xr_ref[tt]
            ni = a_r * ci + a_i * cr + xi_ref[tt]
            sr_ref[tt] = nr
            si_ref[tt] = ni
            return nr, ni

        cr, ci = lax.fori_loop(0, lc, step, (st_ref[0], st_ref[1]), unroll=8)
        st_ref[0] = cr
        st_ref[1] = ci

    blk = pl.BlockSpec((lc, S5_ROWS, LANE), lambda g, ids_ref, ups_ref: (ids_ref[g], 0, 0))
    par = pl.BlockSpec((S5_ROWS, LANE), lambda g, ids_ref, ups_ref: (0, 0))
    return pl.pallas_call(
        body, name=name,
        grid_spec=pltpu.PrefetchScalarGridSpec(
            num_scalar_prefetch=2, grid=(len(order),), in_specs=[blk, blk, par, par], out_specs=[blk, blk],
            scratch_shapes=[pltpu.VMEM((2, S5_ROWS, LANE), F32)]),
        out_shape=[jax.ShapeDtypeStruct(xr.shape, F32), jax.ShapeDtypeStruct(xr.shape, F32)],
        compiler_params=_params(("arbitrary",)),
    )(ids, ups, xr, xi, ar, ai)


def s5_da_call(lr, li, pr, pi, lc):
    t = lr.shape[0]

    def body(lr_ref, li_ref, pr_ref, pi_ref, dar_ref, dai_ref):
        @pl.when(pl.program_id(0) == 0)
        def _():
            dar_ref[...] = jnp.zeros_like(dar_ref)
            dai_ref[...] = jnp.zeros_like(dai_ref)

        a, b, c, e = lr_ref[...], li_ref[...], pr_ref[...], pi_ref[...]
        dar_ref[...] += jnp.sum(a * c + b * e, axis=0)
        dai_ref[...] += jnp.sum(b * c - a * e, axis=0)

    blk = pl.BlockSpec((lc, S5_ROWS, LANE), lambda g: (g, 0, 0))
    par = pl.BlockSpec((S5_ROWS, LANE), lambda g: (0, 0))
    return pl.pallas_call(
        body, name='s5_da', grid=(t // lc,), in_specs=[blk] * 4, out_specs=[par, par],
        out_shape=[jax.ShapeDtypeStruct((S5_ROWS, LANE), F32)] * 2,
        compiler_params=_params(("arbitrary",)),
    )(lr, li, pr, pi)


def make_s5_scan(direction, n_chunks, lc):
    orders = _s5_orders(n_chunks)
    order, back = orders[direction], orders[2 + direction]
    chain = np.concatenate([np.arange(j * lc, (j + 1) * lc)[::1 if up else -1] for j, up in order])
    prev = np.full((n_chunks * lc,), n_chunks * lc, np.int32)
    prev[chain[1:]] = chain[:-1]

    @jax.custom_vjp
    def scan(xr, xi, ar, ai):
        return tuple(s5_scan_call(xr, xi, ar, ai, order, lc, 's5_scan_d%d' % direction))

    def fwd(xr, xi, ar, ai):
        sr, si = s5_scan_call(xr, xi, ar, ai, order, lc, 's5_scan_d%d' % direction)
        return (sr, si), (sr, si, ar, ai)

    def bwd(res, g):
        sr, si, ar, ai = res
        lr, li = s5_scan_call(g[0], g[1], ar, -ai, back, lc, 's5_adjoint_d%d' % direction)
        zero = jnp.zeros((1,) + sr.shape[1:], F32)
        pr = jnp.take(jnp.concatenate([sr, zero]), prev, axis=0)
        pi = jnp.take(jnp.concatenate([si, zero]), prev, axis=0)
        dar, dai = s5_da_call(lr, li, pr, pi, lc)
        return lr, li, dar, dai

    scan.defvjp(fwd, bwd)
    return scan


def loss_fwd_call(y, target, tile):
    n, dm = y.shape

    def body(y_ref, t_ref, o_ref):
        e = y_ref[...] - t_ref[...]
        o_ref[...] = jnp.full(o_ref.shape, 0.5 * jnp.sum(e * e) / dm, F32)

    out = pl.pallas_call(
        body, name='loss_fwd', grid=(n // tile,),
        in_specs=[pl.BlockSpec((tile, dm), lambda i: (i, 0))] * 2,
        out_specs=pl.BlockSpec((8, LANE), lambda i: (i, 0)),
        out_shape=jax.ShapeDtypeStruct((8 * (n // tile), LANE), F32),
        compiler_params=_params(("parallel",)),
    )(y, target)
    return jnp.sum(out[::8, 0])


def make_loss(tile):
    @jax.custom_vjp
    def loss(y, target):
        return loss_fwd_call(y, target, tile)

    def fwd(y, target):
        return loss_fwd_call(y, target, tile), (y, target)

    def bwd(res, g):
        y, target = res
        scale = jnp.full((1, y.shape[1]), g / y.shape[1], F32)

        def body(y_ref, t_ref, s_ref, o_ref):
            o_ref[...] = (y_ref[...] - t_ref[...]) * s_ref[...]

        dy = pl.pallas_call(
            body, name='loss_bwd', grid=(y.shape[0] // tile,),
            in_specs=[pl.BlockSpec((tile, y.shape[1]), lambda i: (i, 0))] * 2 +
                     [pl.BlockSpec((1, y.shape[1]), lambda i: (0, 0))],
            out_specs=pl.BlockSpec((tile, y.shape[1]), lambda i: (i, 0)),
            out_shape=jax.ShapeDtypeStruct(y.shape, F32),
            compiler_params=_params(("parallel",)),
        )(y, target, scale)
        return dy, jnp.zeros_like(target)

    loss.defvjp(fwd, bwd)
    return loss


def _coords():
    return lax.axis_index("x"), lax.axis_index("y"), lax.axis_index("c")


def all_gather(x, name):
    def body(x_ref, out_ref, send_sems, recv_sems, local_sem):
        x, y, c = _coords()
        me, sibling = (x, y, c), (x, y, 1 - c)
        chips = [(1 - x, y), (x, 1 - y), (1 - x, 1 - y)]

        def slot(px, py, pc):
            return out_ref.at[4 * px + 2 * py + pc]

        def copy(k, block, to, src=None):
            return pltpu.make_async_remote_copy(
                src_ref=slot(*block) if src is None else src, dst_ref=slot(*block),
                send_sem=send_sems.at[k], recv_sem=recv_sems.at[k], device_id=to, device_id_type=MESH)

        mine = pltpu.make_async_copy(x_ref, slot(*me), local_sem)
        mine.start()
        first = [copy(0, me, sibling, src=x_ref)]
        first += [copy(1 + j, me, (*chip, c), src=x_ref) for j, chip in enumerate(chips)]
        for cp in first:
            cp.start()
        passed = [copy(4 + j, (*chip, c), sibling) for j, chip in enumerate(chips)]
        for j, chip in enumerate(chips):
            copy(1 + j, (*chip, c), me).wait_recv()
            passed[j].start()
        copy(0, sibling, me).wait_recv()
        for j, chip in enumerate(chips):
            copy(4 + j, (*chip, 1 - c), me).wait_recv()
        for cp in first + passed:
            cp.wait_send()
        mine.wait()

    return pl.pallas_call(
        body, name=name, out_shape=jax.ShapeDtypeStruct((N_DEV,) + x.shape, x.dtype),
        in_specs=[pl.BlockSpec(memory_space=pl.ANY)], out_specs=pl.BlockSpec(memory_space=pl.ANY),
        scratch_shapes=[pltpu.SemaphoreType.DMA((7,)), pltpu.SemaphoreType.DMA((7,)), pltpu.SemaphoreType.DMA],
    )(x)


def all_to_all(x, name):
    def body(x_ref, out_ref, send_sems, recv_sems, local_sem):
        x, y, c = _coords()
        me = 4 * x + 2 * y + c
        mine = pltpu.make_async_copy(x_ref.at[me], out_ref.at[me], local_sem)
        mine.start()
        copies = []
        for k in range(1, N_DEV):
            px = 1 - x if k & 4 else x
            py = 1 - y if k & 2 else y
            pc = 1 - c if k & 1 else c
            peer = 4 * px + 2 * py + pc
            copies.append(pltpu.make_async_remote_copy(
                src_ref=x_ref.at[peer], dst_ref=out_ref.at[me], send_sem=send_sems.at[k - 1],
                recv_sem=recv_sems.at[k - 1], device_id=(px, py, pc), device_id_type=MESH))
        for cp in copies:
            cp.start()
        for cp in copies:
            cp.wait_recv()
        for cp in copies:
            cp.wait_send()
        mine.wait()

    return pl.pallas_call(
        body, name=name, out_shape=jax.ShapeDtypeStruct(x.shape, x.dtype),
        in_specs=[pl.BlockSpec(memory_space=pl.ANY)], out_specs=pl.BlockSpec(memory_space=pl.ANY),
        scratch_shapes=[pltpu.SemaphoreType.DMA((7,)), pltpu.SemaphoreType.DMA((7,)), pltpu.SemaphoreType.DMA],
    )(x)


def sum_parts_call(parts, name):
    p, r, c = parts.shape
    tr = _pick(r, 256, 8)

    def body(p_ref, o_ref):
        acc = p_ref[0].astype(F32)
        for j in range(1, p):
            acc = acc + p_ref[j].astype(F32)
        o_ref[...] = acc

    return pl.pallas_call(
        body, name=name, grid=(r // tr,), in_specs=[pl.BlockSpec((p, tr, c), lambda i: (0, i, 0))],
        out_specs=pl.BlockSpec((tr, c), lambda i: (i, 0)), out_shape=jax.ShapeDtypeStruct((r, c), F32),
        compiler_params=_params(("parallel",)),
    )(parts)


@jax.custom_vjp
def gather_rows(x):
    return all_gather(x, 'mod_gather')


def _gather_rows_fwd(x):
    return all_gather(x, 'mod_gather'), None


def _gather_rows_bwd(_, ct):
    return (sum_parts_call(all_to_all(ct, 'mod_scatter'), 'mod_scatter_sum'),)


gather_rows.defvjp(_gather_rows_fwd, _gather_rows_bwd)


def adamw_call(parts, w, m, v, name):
    p, r, c = parts.shape
    tr = _pick(r, 128, 8)
    c1 = 1.0 / (1.0 - ADAM_B1 ** ADAM_STEP)
    c2 = 1.0 / (1.0 - ADAM_B2 ** ADAM_STEP)

    def body(p_ref, w_ref, m_ref, v_ref, g_ref, d_ref, m2_ref, v2_ref):
        g = p_ref[0].astype(F32)
        for j in range(1, p):
            g = g + p_ref[j].astype(F32)
        m2 = ADAM_B1 * m_ref[...] + (1.0 - ADAM_B1) * g
        v2 = ADAM_B2 * v_ref[...] + (1.0 - ADAM_B2) * (g * g)
        g_ref[...] = g
        m2_ref[...] = m2
        v2_ref[...] = v2
        d_ref[...] = -ADAM_LR * ((m2 * c1) / (jnp.sqrt(v2 * c2) + ADAM_EPS) + ADAM_WD * w_ref[...])

    blk = pl.BlockSpec((tr, c), lambda i: (i, 0))
    return pl.pallas_call(
        body, name=name, grid=(r // tr,),
        in_specs=[pl.BlockSpec((p, tr, c), lambda i: (0, i, 0)), blk, blk, blk], out_specs=[blk] * 4,
        out_shape=[jax.ShapeDtypeStruct((r, c), F32)] * 4,
        compiler_params=_params(("parallel",)),
    )(parts, w, m, v)


def _deinterleave(z, heads):
    lead = z.shape[:-1]
    return z.reshape(lead + (heads, ATT_D // 2, 2)).swapaxes(-1, -2).reshape(lead + (heads * ATT_D,))


def _unshard_cols(g):
    return jnp.moveaxis(g, 0, -2).reshape(g.shape[1:-1] + (N_DEV * g.shape[-1],))


def _unshard_rows(g):
    return jnp.moveaxis(g, 0, 1).reshape((g.shape[1], N_DEV * g.shape[2]) + g.shape[3:])


def _pack_w_in(g, dm):
    w = _unshard_cols(g)
    splits = (GLA_QK, GLA_QK, GLA_V, GLA_V, GLA_RANK, S5_WIDTH, ATT_Q, ATT_KV, ATT_KV, 3 * dm)
    gq, gk, gv, gr, glr, su, aq, ak, av, bg = jnp.split(w, np.cumsum(splits)[:-1].tolist(), axis=-1)
    pad = jnp.zeros(w.shape[:-1] + (LANE - GLA_RANK,), w.dtype)
    return jnp.concatenate([gq, gk, gv, gr, su, _deinterleave(aq, ATT_QH), _deinterleave(ak, ATT_KVH),
                            av, bg, glr, pad], axis=-1)


def _in_widths(dm):
    return (GLA_QK, GLA_QK, GLA_V, GLA_V, S5_WIDTH, ATT_Q, ATT_KV, ATT_KV, 3 * dm, LANE)


def _make_split(widths):
    cuts = np.cumsum(widths)[:-1].tolist()

    @jax.custom_vjp
    def split(z):
        return tuple(jnp.split(z, cuts, axis=1))

    def fwd(z):
        return tuple(jnp.split(z, cuts, axis=1)), None

    def bwd(_, cots):
        return (jnp.concatenate(cots, axis=1),)

    split.defvjp(fwd, bwd)
    return split


def _block_diag_in(b):
    eye = jnp.eye(S5_GROUPS, dtype=F32)
    return jnp.einsum('gpc,gh->gchp', b, eye).reshape(S5_WIDTH, S5_CH)


def _block_diag_out(cm):
    eye = jnp.eye(S5_GROUPS, dtype=F32)
    return jnp.einsum('gcp,gh->gphc', cm, eye).reshape(S5_CH, S5_WIDTH)


def _s5_discretise(lam_re, lam_im, log_dt):
    dt = jnp.exp(log_dt)[:, None]
    mag = jnp.exp(lam_re * dt)
    a_re, a_im = mag * jnp.cos(lam_im * dt), mag * jnp.sin(lam_im * dt)
    den = lam_re * lam_re + lam_im * lam_im
    nr, ni = a_re - 1, a_im
    k_re = (nr * lam_re + ni * lam_im) / den
    k_im = (ni * lam_re - nr * lam_im) / den
    return a_re, a_im, k_re, k_im


def _rope_tables(n_ctx, n_lat):
    rows = jnp.repeat(jnp.arange(n_lat // GRID_W), GRID_W).astype(F32)
    cols = jnp.tile(jnp.arange(GRID_W), n_lat // GRID_W).astype(F32)
    n_freq = ATT_D // 4
    inv = ROPE_THETA ** (-jnp.arange(n_freq, dtype=F32) / n_freq)
    ang = jnp.concatenate([rows[:, None] * inv, cols[:, None] * inv], -1)
    cos, sin = jnp.cos(ang), jnp.sin(ang)
    cosf = jnp.concatenate([jnp.ones((n_ctx, ATT_D), F32), jnp.concatenate([cos, cos], -1)], 0)
    sinf = jnp.concatenate([jnp.zeros((n_ctx, ATT_D), F32), jnp.concatenate([-sin, sin], -1)], 0)
    return cosf, sinf


def _build_ops(n, n_ctx, dm, dff, depth):
    t256 = min(256, n_ctx)
    t64 = 64
    alpha = (2 * depth) ** 0.25
    ops = {}

    def seg(i, tile, pc, pl_):
        return jnp.where(i < n_ctx // tile, pc, pl_)

    def modulate_f(i, rows, params):
        sh = seg(i, t256, params[0], params[1])
        sc = seg(i, t256, params[2], params[3])
        return (rows[0] * (1.0 + sc) + sh,)

    def postnorm_f(i, rows, params):
        g = seg(i, t256, params[0], params[1])
        return (_layer_norm(alpha * rows[0] + g * rows[1], params[2], params[3]),)

    def gla_out_f(i, rows, params):
        o = rows[0] + rows[1]
        heads = [_layer_norm(o[:, h * GLA_DV:(h + 1) * GLA_DV], None, None) for h in range(GLA_HEADS)]
        return (jnp.concatenate(heads, axis=1) * params[0] * _silu(rows[2]),)

    def s5_in_f(i, rows, params):
        br, bi = rows[0][:, :S5_CH], rows[0][:, S5_CH:]
        kr0, ki0, kr1, ki1 = params
        return (kr0 * br - ki0 * bi, kr0 * bi + ki0 * br, kr1 * br - ki1 * bi, kr1 * bi + ki1 * br)

    def s5_mid_f(i, rows, params):
        return (jax.nn.gelu(rows[0] + rows[1] * params[0]),)

    def s5_gate_f(i, rows, params):
        return (rows[0] * jax.nn.sigmoid(rows[1]),)

    def qk_f(swap):
        def f(i, rows, params):
            aq, ak, cosf, sinf = rows
            outs = []
            for z, w, heads in ((aq, params[0], ATT_QH), (ak, params[1], ATT_KVH)):
                hs = []
                for h in range(heads):
                    zh = z[:, h * ATT_D:(h + 1) * ATT_D]
                    zn = zh * lax.rsqrt(jnp.mean(zh * zh, -1, keepdims=True) + EPS) * w
                    hs.append(zn * cosf + swap(zn) * sinf)
                outs.append(jnp.concatenate(hs, axis=1))
            return tuple(outs)
        return f

    def silu_f(i, rows, params):
        return (_silu(rows[0]),)

    def merge_f(i, rows, params):
        pg, ps, pa, bg = rows
        gate = jax.nn.sigmoid(bg)
        return (gate[:, :dm] * pg + gate[:, dm:2 * dm] * ps + gate[:, 2 * dm:] * pa,)

    def swiglu_f(i, rows, params):
        return (_silu(rows[0]) * rows[1],)

    ops['modulate1'] = make_rowwise('modulate1', modulate_f, (dm,), t256)
    ops['modulate2'] = make_rowwise('modulate2', modulate_f, (dm,), t256)
    ops['postnorm1'] = make_rowwise('postnorm1', postnorm_f, (dm,), t256)
    ops['postnorm2'] = make_rowwise('postnorm2', postnorm_f, (dm,), t256)
    ops['gla_out'] = make_rowwise('gla_out', gla_out_f, (GLA_V,), t256)
    ops['s5_in'] = make_rowwise('s5_in', s5_in_f, (S5_CH,) * 4, t64)
    ops['s5_mid'] = make_rowwise('s5_mid', s5_mid_f, (S5_WIDTH,), t256)
    ops['s5_gate'] = make_rowwise('s5_gate', s5_gate_f, (S5_WIDTH,), t256)
    ops['merge'] = make_rowwise('merge', merge_f, (dm,), t64)
    ops['swiglu'] = make_rowwise('swiglu', swiglu_f, (dff,), t64)
    ops['qk_prep'] = make_rowwise('qk_prep', qk_f(_swap_plain), (ATT_Q, ATT_KV), t256, n_nondiff=2,
                                  f_diff=qk_f(_swap_halves))
    ops['silu_c'] = make_rowwise('silu_c', silu_f, (dm,), 16)
    return ops


def _flat2d(a):
    return a.reshape(-1, a.shape[-1])


def _step(given):
    x, c, ctx = given['x'][0], given['c'], given['ctx'][0]
    target = given['loss_target'][0]
    n_lat, dm = x.shape
    n_ctx = ctx.shape[0]
    n = n_ctx + n_lat
    depth = given['w_ada'].shape[0]
    dff = given['w_ffn_out'].shape[1] * N_DEV
    ada_cols = given['w_ada'].shape[2]
    xi, yi, ci = _coords()
    me = 4 * xi + 2 * yi + ci
    lc = n_ctx
    ops = _build_ops(n, n_ctx, dm, dff, depth)
    names = ['ada', 'in', 's5b', 's5c', 'glu', 'pg', 'ps', 'pa', 'out', 'fa', 'fb', 'fo']
    lin = {k: make_linear('lin_' + k) for k in names}
    split_in = _make_split(_in_widths(dm))
    gla = make_gla(n_ctx)
    attention = make_attention(n_ctx)
    scans = [make_s5_scan(d, n // lc, lc) for d in range(2)]
    loss_op = make_loss(min(256, n_lat))
    cosf, sinf = _rope_tables(n_ctx, n_lat)

    big = [k for k in GATHERED if k not in ('w_gla_gate', 'b_gla_gate')]
    gathered = {k: all_gather(given[k].astype(BF16) if k in big else given[k], 'gather_' + k) for k in GATHERED}
    c_all = all_gather(c, 'gather_c').reshape(N_DEV, dm)

    diff0 = {
        'x': x,
        'rep': {k: given[k] for k in REPLICATED},
        'proxy': {k: jnp.zeros((N_DEV,) + given[k].shape, F32) for k in GATHERED},
        'ada': jnp.zeros(given['w_ada'].shape, F32),
    }
    w_ada_bf = given['w_ada'].astype(BF16)

    def loss_fn(diff):
        r, prox = diff['rep'], diff['proxy']

        def both(k, pack):
            return pack(gathered[k]), pack(prox[k])

        w_in, p_in = both('w_in', lambda g: _pack_w_in(g, dm))
        w_fi, p_fi = both('w_ffn_in', _unshard_cols)
        w_fo, p_fo = both('w_ffn_out', _unshard_rows)
        w_o, p_o = both('w_out', _unshard_rows)
        w_pg, p_pg = both('w_proj_gla', _unshard_cols)
        w_ps, p_ps = both('w_proj_s5', _unshard_cols)
        w_pa, p_pa = both('w_proj_attn', _unshard_cols)
        w_glu, p_glu = both('w_s5_glu', _unshard_rows)
        wgate = _unshard_cols(gathered['w_gla_gate'] + prox['w_gla_gate'])
        wgate = jnp.pad(wgate, ((0, 0), (0, 0), (0, LANE - GLA_RANK), (0, 0)))
        bgate = _unshard_cols(gathered['b_gla_gate'] + prox['b_gla_gate'])[:, :, None, :]

        disc = jax.vmap(jax.vmap(_s5_discretise))(r['s5_lam_re'], r['s5_lam_im'], r['s5_log_dt'])
        a_re, a_im = (z.reshape(depth, 2, S5_ROWS, LANE) for z in disc[:2])
        k_re, k_im = (z.reshape(depth, 2, 1, S5_CH) for z in disc[2:])
        b_full = jnp.concatenate([jax.vmap(_block_diag_in)(r['s5_b_re']), jax.vmap(_block_diag_in)(r['s5_b_im'])], -1)
        bd_out = jax.vmap(_block_diag_out)
        c_full = jnp.concatenate([bd_out(r['s5_c_re'][:, 0]), -bd_out(r['s5_c_im'][:, 0]),
                                  bd_out(r['s5_c_re'][:, 1]), -bd_out(r['s5_c_im'][:, 1])], 1)
        row = lambda z: z[:, None, :]
        qw, kw = row(_deinterleave(r['q_norm_w'], 1)), row(_deinterleave(r['k_norm_w'], 1))

        c16 = jnp.concatenate([c_all, r['c_ctx'][None], jnp.zeros((7, dm), F32)], 0)
        c16s, = ops['silu_c']((c16,), ())

        xs = dict(w_ada=w_ada_bf, p_ada=diff['ada'], b_ada=r['b_ada'], w_in=w_in, p_in=p_in,
                  w_fa=w_fi[..., :dff], p_fa=p_fi[..., :dff], w_fb=w_fi[..., dff:], p_fb=p_fi[..., dff:],
                  w_fo=w_fo, p_fo=p_fo, w_o=w_o, p_o=p_o, w_pg=w_pg, p_pg=p_pg, w_ps=w_ps, p_ps=p_ps,
                  w_pa=w_pa, p_pa=p_pa, w_glu=w_glu, p_glu=p_glu, wgate=wgate, bgate=bgate,
                  a_re=a_re, a_im=a_im, k_re=k_re, k_im=k_im, b_full=b_full, c_full=c_full, qw=qw, kw=kw,
                  gnw=row(r['gla_norm_w']), s5_d=row(r['s5_d']), ln1_w=row(r['ln1_w']), ln1_b=row(r['ln1_b']),
                  ln2_w=row(r['ln2_w']), ln2_b=row(r['ln2_b']))

        def layer(xc, p):
            msh = lin['ada'](c16s, p['w_ada'], p['p_ada']) + lax.dynamic_slice(p['b_ada'], (me * ada_cols,), (ada_cols,))
            mods = gather_rows(msh)
            m_lat = lax.dynamic_index_in_dim(mods, me, axis=1, keepdims=False).reshape(1, 6 * dm)
            m_ctx = mods[:, N_DEV, :].reshape(1, 6 * dm)
            sh1, sc1, g1, sh2, sc2, g2 = [(m_ctx[:, j * dm:(j + 1) * dm], m_lat[:, j * dm:(j + 1) * dm]) for j in range(6)]

            h, = ops['modulate1']((xc,), sh1 + sc1)
            gq, gk, gv, gr, su, aq, ak, av, bgt, glr = split_in(lin['in'](h, p['w_in'], p['p_in']))
            o2 = gla(gq, gk, gv, glr, p['wgate'], p['bgate'])
            o_gla, = ops['gla_out']((o2[0], o2[1], gr), (p['gnw'],))

            bu = lin['s5b'](su, p['b_full'].astype(BF16), p['b_full'])
            xin = ops['s5_in']((bu,), (p['k_re'][0], p['k_im'][0], p['k_re'][1], p['k_im'][1]))
            tile3 = lambda z: z.reshape(n, S5_ROWS, LANE)
            st = []
            for d in range(2):
                st += scans[d](tile3(xin[2 * d]), tile3(xin[2 * d + 1]), p['a_re'][d], p['a_im'][d])
            states = jnp.concatenate([z.reshape(n, S5_CH) for z in st], axis=1)
            y0 = lin['s5c'](states, p['c_full'].astype(BF16), p['c_full'])
            ya, = ops['s5_mid']((y0, su), (p['s5_d'],))
            o_s5, = ops['s5_gate']((ya, lin['glu'](ya, p['w_glu'], p['p_glu'])), ())

            q, k = ops['qk_prep']((aq, ak, cosf, sinf), (p['qw'], p['kw']))
            o_att = attention(q, k, av)

            merged, = ops['merge']((lin['pg'](o_gla, p['w_pg'], p['p_pg']), lin['ps'](o_s5, p['w_ps'], p['p_ps']),
                                    lin['pa'](o_att, p['w_pa'], p['p_pa']), bgt), ())
            mix = lin['out'](merged, p['w_o'], p['p_o'])
            x1, = ops['postnorm1']((xc, mix), g1 + (p['ln1_w'], p['ln1_b']))
            h2, = ops['modulate2']((x1,), sh2 + sc2)
            act, = ops['swiglu']((lin['fa'](h2, p['w_fa'], p['p_fa']), lin['fb'](h2, p['w_fb'], p['p_fb'])), ())
            x2, = ops['postnorm2']((x1, lin['fo'](act, p['w_fo'], p['p_fo'])), g2 + (p['ln2_w'], p['ln2_b']))
            return x2, None

        x_out, _ = lax.scan(layer, jnp.concatenate([ctx, diff['x']], 0), xs)
        return loss_op(x_out[n_ctx:], target)

    loss_local, grads = jax.value_and_grad(loss_fn)(diff0)
    loss = lax.psum(loss_local, ("x", "y", "c"))

    out = {}

    def update(k, parts):
        w2 = _flat2d(given[k])
        res = adamw_call(parts.reshape((parts.shape[0],) + w2.shape), w2, _flat2d(given['m_' + k]),
                         _flat2d(given['v_' + k]), 'adamw_' + k)
        out[k] = [z.reshape(given[k].shape) for z in res]

    for k in GATHERED:
        update(k, all_to_all(grads['proxy'][k].astype(BF16), 'scatter_' + k))
    update('w_ada', grads['ada'][None])

    sizes = [int(np.prod(given[k].shape)) for k in REPLICATED]
    total = sum(sizes)
    padded = -(-total // (8 * LANE)) * (8 * LANE)

    def flat(prefix, src):
        v = jnp.concatenate([src[prefix + k].reshape(-1) for k in REPLICATED] + [jnp.zeros((padded - total,), F32)])
        return v.reshape(padded // LANE, LANE)

    g_all = all_gather(flat('', grads['rep']), 'gather_small_grads')
    res = adamw_call(g_all, flat('', given), flat('m_', given), flat('v_', given), 'adamw_small')
    offs = np.cumsum([0] + sizes)
    for j, k in enumerate(REPLICATED):
        out[k] = [z.reshape(-1)[offs[j]:offs[j + 1]].reshape(given[k].shape) for z in res]

    grad_x = grads['x'][None]
    return (loss, grad_x) + tuple(out[k][j] for j in range(4) for k in WEIGHTS)


def kernel(x, c, ctx, c_ctx, w_ada, b_ada, w_in, w_gla_gate, b_gla_gate, gla_norm_w, s5_lam_re, s5_lam_im, s5_log_dt, s5_b_re, s5_b_im, s5_c_re, s5_c_im, s5_d, w_s5_glu, q_norm_w, k_norm_w, w_proj_gla, w_proj_s5, w_proj_attn, w_out, ln1_w, ln1_b, ln2_w, ln2_b, w_ffn_in, w_ffn_out, loss_target, m_c_ctx, m_w_ada, m_b_ada, m_w_in, m_w_gla_gate, m_b_gla_gate, m_gla_norm_w, m_s5_lam_re, m_s5_lam_im, m_s5_log_dt, m_s5_b_re, m_s5_b_im, m_s5_c_re, m_s5_c_im, m_s5_d, m_w_s5_glu, m_q_norm_w, m_k_norm_w, m_w_proj_gla, m_w_proj_s5, m_w_proj_attn, m_w_out, m_ln1_w, m_ln1_b, m_ln2_w, m_ln2_b, m_w_ffn_in, m_w_ffn_out, v_c_ctx, v_w_ada, v_b_ada, v_w_in, v_w_gla_gate, v_b_gla_gate, v_gla_norm_w, v_s5_lam_re, v_s5_lam_im, v_s5_log_dt, v_s5_b_re, v_s5_b_im, v_s5_c_re, v_s5_c_im, v_s5_d, v_w_s5_glu, v_q_norm_w, v_k_norm_w, v_w_proj_gla, v_w_proj_s5, v_w_proj_attn, v_w_out, v_ln1_w, v_ln1_b, v_ln2_w, v_ln2_b, v_w_ffn_in, v_w_ffn_out):
    return _step(dict(locals()))
```

```python
import functools
import math

import numpy as np
import jax
import jax.numpy as jnp
from jax import lax
from jax.experimental import pallas as pl
from jax.experimental.pallas import tpu as pltpu

F32 = jnp.float32
BF16 = jnp.bfloat16
MESH = pl.DeviceIdType.MESH
N_DEV = 8
VMEM_LIMIT_V7X = 48 * 1024 * 1024
LANE = 128

GRID_W = 64
GLA_HEADS, GLA_DK, GLA_DV = 4, 128, 256
GLA_QK, GLA_V = GLA_HEADS * GLA_DK, GLA_HEADS * GLA_DV
GLA_RANK, GLA_TAU, GLA_CHUNK = 16, 16.0, 64
S5_WIDTH, S5_GROUP, S5_GROUPS, S5_STATE = 768, 16, 48, 64
S5_CH = S5_GROUPS * S5_STATE
S5_ROWS = S5_CH // LANE
ATT_QH, ATT_KVH, ATT_D = 8, 2, 128
ATT_Q, ATT_KV = ATT_QH * ATT_D, ATT_KVH * ATT_D
ROPE_THETA = 10000.0
EPS = 1e-6
ADAM_LR, ADAM_B1, ADAM_B2, ADAM_EPS, ADAM_WD, ADAM_STEP = 0.001, 0.9, 0.999, 1e-08, 0.01, 10

WEIGHTS = ['c_ctx', 'w_ada', 'b_ada', 'w_in', 'w_gla_gate', 'b_gla_gate', 'gla_norm_w', 's5_lam_re',
           's5_lam_im', 's5_log_dt', 's5_b_re', 's5_b_im', 's5_c_re', 's5_c_im', 's5_d', 'w_s5_glu',
           'q_norm_w', 'k_norm_w', 'w_proj_gla', 'w_proj_s5', 'w_proj_attn', 'w_out', 'ln1_w', 'ln1_b',
           'ln2_w', 'ln2_b', 'w_ffn_in', 'w_ffn_out']
GATHERED = ['w_in', 'w_gla_gate', 'b_gla_gate', 'w_s5_glu', 'w_proj_gla', 'w_proj_s5', 'w_proj_attn',
            'w_out', 'w_ffn_in', 'w_ffn_out']
REPLICATED = [n for n in WEIGHTS if n not in GATHERED and n != 'w_ada']
MISC = ['w_out', 'w_proj_gla', 'w_proj_s5', 'w_proj_attn', 'w_s5_glu']
FAMILIES = ['w_in', 'w_ffn_in', 'w_ffn_out', 'misc']


def _params(sem=None):
    return pltpu.CompilerParams(dimension_semantics=sem, vmem_limit_bytes=VMEM_LIMIT_V7X)


def _pick(dim, target, align):
    best = None
    t = align
    while t <= min(dim, target):
        if dim % t == 0:
            best = t
        t += align
    return dim if best is None else best


_CONTRACT = {'nn': ((1,), (0,)), 'nt': ((1,), (1,)), 'tn': ((0,), (0,))}


def _dot(a, b, dims):
    return lax.dot_general(a.astype(BF16), b.astype(BF16), (_CONTRACT[dims], ((), ())),
                           preferred_element_type=F32)


@functools.partial(jax.custom_vjp, nondiff_argnums=(2,))
def _mm(a, b, dims):
    return _dot(a, b, dims)


def _mm_fwd(a, b, dims):
    return _dot(a, b, dims), (a, b)


def _mm_bwd(dims, res, g):
    a, b = res
    if dims == 'nn':
        return _dot(g, b, 'nt'), _dot(a, g, 'tn')
    if dims == 'nt':
        return _dot(g, b, 'nn'), _dot(g, a, 'tn')
    return _dot(b, g, 'nt'), _dot(a, g, 'nn')


_mm.defvjp(_mm_fwd, _mm_bwd)


@jax.custom_vjp
def _tri_dot(tri, tri_t, z):
    return lax.dot_general(tri, z, (((1,), (0,)), ((), ())), precision=lax.Precision.HIGHEST,
                           preferred_element_type=F32)


def _tri_dot_fwd(tri, tri_t, z):
    return _tri_dot(tri, tri_t, z), (tri, tri_t)


def _tri_dot_bwd(res, g):
    tri, tri_t = res
    dz = lax.dot_general(tri_t, g, (((1,), (0,)), ((), ())), precision=lax.Precision.HIGHEST,
                         preferred_element_type=F32)
    return jnp.zeros_like(tri), jnp.zeros_like(tri_t), dz


_tri_dot.defvjp(_tri_dot_fwd, _tri_dot_bwd)


def _tri_dot_plain(tri, tri_t, z):
    return lax.dot_general(tri, z, (((1,), (0,)), ((), ())), precision=lax.Precision.HIGHEST,
                           preferred_element_type=F32)


@jax.custom_vjp
def _swap_halves(x):
    return pltpu.roll(x, LANE // 2, 1)


def _swap_fwd(x):
    return pltpu.roll(x, LANE // 2, 1), None


def _swap_bwd(_, g):
    return (pltpu.roll(g, LANE // 2, 1),)


_swap_halves.defvjp(_swap_fwd, _swap_bwd)


def _swap_plain(x):
    return pltpu.roll(x, LANE // 2, 1)


def _coords():
    return lax.axis_index("x"), lax.axis_index("y"), lax.axis_index("c")


def _exchange_copies(kind, x_ref, out_ref, send_sems, recv_sems, local_sem):
    x, y, c = _coords()
    me = 4 * x + 2 * y + c
    local = pltpu.make_async_copy(x_ref if kind == 'gather' else x_ref.at[me], out_ref.at[me], local_sem)
    copies = []
    for k in range(1, N_DEV):
        px = 1 - x if k & 4 else x
        py = 1 - y if k & 2 else y
        pc = 1 - c if k & 1 else c
        src = x_ref if kind == 'gather' else x_ref.at[4 * px + 2 * py + pc]
        copies.append(pltpu.make_async_remote_copy(
            src_ref=src, dst_ref=out_ref.at[me], send_sem=send_sems.at[k - 1], recv_sem=recv_sems.at[k - 1],
            device_id=(px, py, pc), device_id_type=MESH))
    return local, copies


def _exchange_start(*refs):
    local, copies = _exchange_copies(*refs)
    local.start()
    for cp in copies:
        cp.start()


def _exchange_wait(*refs):
    local, copies = _exchange_copies(*refs)
    for cp in copies:
        cp.wait_recv()
    for cp in copies:
        cp.wait_send()
    local.wait()


_EXCHANGE_SEMS = [pltpu.SemaphoreType.DMA((N_DEV - 1,)), pltpu.SemaphoreType.DMA((N_DEV - 1,)),
                  pltpu.SemaphoreType.DMA]


def pmatmul(a, b, dims, name, exchange=None):
    if dims == 'nn':
        (m, k), (k2, n) = a.shape, b.shape
    elif dims == 'nt':
        (m, k), (n, k2) = a.shape, b.shape
    else:
        (k, m), (k2, n) = a.shape, b.shape
    assert k == k2, (a.shape, b.shape, dims)
    tm = _pick(m, 768, LANE if dims == 'tn' else 8)
    tn = _pick(n, 1024, LANE)
    tk = _pick(k, 1024, LANE)
    nk = k // tk
    a_spec = pl.BlockSpec((tk, tm), lambda i, j, kk: (kk, i)) if dims == 'tn' else \
        pl.BlockSpec((tm, tk), lambda i, j, kk: (i, kk))
    b_spec = pl.BlockSpec((tn, tk), lambda i, j, kk: (j, kk)) if dims == 'nt' else \
        pl.BlockSpec((tk, tn), lambda i, j, kk: (kk, j))

    ni, nj = m // tm, n // tn

    def matmul_step(a_ref, b_ref, o_ref, acc_ref):
        kk = pl.program_id(2)

        @pl.when(kk == 0)
        def _():
            acc_ref[...] = jnp.zeros_like(acc_ref)

        acc_ref[...] += _dot(a_ref[...], b_ref[...], dims)

        @pl.when(kk == nk - 1)
        def _():
            o_ref[...] = acc_ref[...]

    out_spec = pl.BlockSpec((tm, tn), lambda i, j, kk: (i, j))
    if exchange is None:
        return pl.pallas_call(
            matmul_step, name=name, grid=(ni, nj, nk),
            in_specs=[a_spec, b_spec], out_specs=out_spec, out_shape=jax.ShapeDtypeStruct((m, n), F32),
            scratch_shapes=[pltpu.VMEM((tm, tn), F32)],
            compiler_params=_params(("parallel", "parallel", "arbitrary")),
        )(a, b)

    kind, x = exchange
    x_shape = (N_DEV,) + x.shape if kind == 'gather' else x.shape

    def body(a_ref, b_ref, x_ref, o_ref, got_ref, acc_ref, send_sems, recv_sems, local_sem):
        i, j, kk = pl.program_id(0), pl.program_id(1), pl.program_id(2)
        refs = (kind, x_ref, got_ref, send_sems, recv_sems, local_sem)

        @pl.when(jnp.logical_and(jnp.logical_and(i == 0, j == 0), kk == 0))
        def _():
            _exchange_start(*refs)

        matmul_step(a_ref, b_ref, o_ref, acc_ref)

        @pl.when(jnp.logical_and(jnp.logical_and(i == ni - 1, j == nj - 1), kk == nk - 1))
        def _():
            _exchange_wait(*refs)

    any_spec = pl.BlockSpec(memory_space=pl.ANY)
    return pl.pallas_call(
        body, name=name, grid=(ni, nj, nk),
        in_specs=[a_spec, b_spec, any_spec], out_specs=[out_spec, any_spec],
        out_shape=[jax.ShapeDtypeStruct((m, n), F32), jax.ShapeDtypeStruct(x_shape, x.dtype)],
        scratch_shapes=[pltpu.VMEM((tm, tn), F32)] + _EXCHANGE_SEMS,
        compiler_params=_params(("arbitrary", "arbitrary", "arbitrary")),
    )(a, b, x)


def make_linear(name):
    @jax.custom_vjp
    def linear(a, w, w_proxy):
        return pmatmul(a, w, 'nn', name + '_fwd')

    def fwd(a, w, w_proxy):
        return pmatmul(a, w, 'nn', name + '_fwd'), (a, w)

    def bwd(res, g):
        a, w = res
        return pmatmul(g, w, 'nt', name + '_da'), jnp.zeros_like(w), pmatmul(a, g, 'tn', name + '_dw')

    linear.defvjp(fwd, bwd)
    return linear


def make_carrier_linear(name):
    def run(a, w, block):
        return pmatmul(a, w, 'nn', name + '_fwd', exchange=('gather', block))

    @jax.custom_vjp
    def carrier(a, w, w_proxy, block, grads):
        out, gathered = run(a, w, block)
        return out, gathered, grads

    def fwd(a, w, w_proxy, block, grads):
        out, gathered = run(a, w, block)
        return (out, gathered, grads), (a, w, block)

    def bwd(res, cts):
        a, w, block = res
        g, _, ct_grads = cts
        dw, scattered = pmatmul(a, g, 'tn', name + '_dw', exchange=('scatter', ct_grads))
        return pmatmul(g, w, 'nt', name + '_da'), jnp.zeros_like(w), dw, jnp.zeros_like(block), scattered

    carrier.defvjp(fwd, bwd)
    return carrier


def make_rowwise(name, f, out_widths, tile, n_nondiff=0, f_diff=None):
    f_diff = f if f_diff is None else f_diff

    def fwd_call(rows, params):
        n = rows[0].shape[0]
        nr, npar = len(rows), len(params)

        def body(*refs):
            i = pl.program_id(0)
            outs = f(i, tuple(r[...] for r in refs[:nr]), tuple(p[...] for p in refs[nr:nr + npar]))
            for o_ref, val in zip(refs[nr + npar:], outs):
                o_ref[...] = val

        return pl.pallas_call(
            body, name=name + '_fwd', grid=(n // tile,),
            in_specs=[pl.BlockSpec((tile, r.shape[1]), lambda i: (i, 0)) for r in rows] +
                     [pl.BlockSpec(p.shape, lambda i: (0, 0)) for p in params],
            out_specs=[pl.BlockSpec((tile, w), lambda i: (i, 0)) for w in out_widths],
            out_shape=[jax.ShapeDtypeStruct((n, w), F32) for w in out_widths],
            compiler_params=_params(("parallel",)),
        )(*rows, *params)

    def bwd_call(rows, params, cots):
        n = rows[0].shape[0]
        nr, npar, nc = len(rows), len(params), len(cots)
        nd = nr - n_nondiff

        def body(*refs):
            i = pl.program_id(0)
            rv = tuple(r[...] for r in refs[:nr])
            pv = tuple(p[...] for p in refs[nr:nr + npar])
            cv = tuple(c[...] for c in refs[nr + npar:nr + npar + nc])
            out_refs = refs[nr + npar + nc:]
            _, vjp = jax.vjp(lambda dr, pp: tuple(f_diff(i, dr + rv[nd:], pp)), rv[:nd], pv)
            drows, dparams = vjp(cv)
            for o_ref, val in zip(out_refs[:nd], drows):
                o_ref[...] = val

            @pl.when(i == 0)
            def _():
                for o_ref in out_refs[nd:]:
                    o_ref[...] = jnp.zeros_like(o_ref)

            for o_ref, val in zip(out_refs[nd:], dparams):
                o_ref[...] += val

        outs = pl.pallas_call(
            body, name=name + '_bwd', grid=(n // tile,),
            in_specs=[pl.BlockSpec((tile, r.shape[1]), lambda i: (i, 0)) for r in rows] +
                     [pl.BlockSpec(p.shape, lambda i: (0, 0)) for p in params] +
                     [pl.BlockSpec((tile, c.shape[1]), lambda i: (i, 0)) for c in cots],
            out_specs=[pl.BlockSpec((tile, r.shape[1]), lambda i: (i, 0)) for r in rows[:nd]] +
                      [pl.BlockSpec(p.shape, lambda i: (0, 0)) for p in params],
            out_shape=[jax.ShapeDtypeStruct(r.shape, F32) for r in rows[:nd]] +
                      [jax.ShapeDtypeStruct(p.shape, F32) for p in params],
            compiler_params=_params(("arbitrary",)),
        )(*rows, *params, *cots)
        drows = tuple(outs[:nd]) + tuple(jnp.zeros_like(r) for r in rows[nd:])
        return drows, tuple(outs[nd:])

    @jax.custom_vjp
    def op(rows, params):
        return tuple(fwd_call(rows, params))

    def fwd(rows, params):
        return tuple(fwd_call(rows, params)), (rows, params)

    def bwd(res, cots):
        rows, params = res
        return bwd_call(rows, params, tuple(cots))

    op.defvjp(fwd, bwd)
    return op


def _silu(x):
    return x * jax.nn.sigmoid(x)


def _layer_norm(z, w, b):
    mu = jnp.mean(z, -1, keepdims=True)
    zc = z - mu
    var = jnp.mean(zc * zc, -1, keepdims=True)
    y = zc * lax.rsqrt(var + EPS)
    return y if w is None else y * w + b


def _softmax_rows(s):
    e = jnp.exp(s - jnp.max(s, -1, keepdims=True))
    return e / jnp.sum(e, -1, keepdims=True)


def attention_fwd_call(q, k, v, n_ctx):
    n = q.shape[0]
    tq = n_ctx
    grp = ATT_QH // ATT_KVH
    scale = ATT_D ** -0.5

    def body(q_ref, k_ref, v_ref, o_ref):
        def run(nk):
            p = _softmax_rows(_dot(q_ref[...], k_ref[0:nk, :], 'nt') * scale)
            o_ref[...] = _dot(p, v_ref[0:nk, :], 'nn')

        pl.when(pl.program_id(2) == 0)(lambda: run(n_ctx))
        pl.when(pl.program_id(2) > 0)(lambda: run(n))

    return pl.pallas_call(
        body, name='attn_fwd', grid=(ATT_KVH, grp, n // tq),
        in_specs=[pl.BlockSpec((tq, ATT_D), lambda h, g, i: (i, h * grp + g)),
                  pl.BlockSpec((n, ATT_D), lambda h, g, i: (0, h)),
                  pl.BlockSpec((n, ATT_D), lambda h, g, i: (0, h))],
        out_specs=pl.BlockSpec((tq, ATT_D), lambda h, g, i: (i, h * grp + g)),
        out_shape=jax.ShapeDtypeStruct((n, ATT_Q), F32),
        compiler_params=_params(("parallel", "parallel", "parallel")),
    )(q, k, v)


def attention_bwd_call(q, k, v, do, n_ctx):
    n = q.shape[0]
    tq = n_ctx
    grp = ATT_QH // ATT_KVH
    scale = ATT_D ** -0.5

    def body(q_ref, k_ref, v_ref, do_ref, dq_ref, dk_ref, dv_ref):
        @pl.when(jnp.logical_and(pl.program_id(1) == 0, pl.program_id(2) == 0))
        def _():
            dk_ref[...] = jnp.zeros_like(dk_ref)
            dv_ref[...] = jnp.zeros_like(dv_ref)

        def run(nk):
            qv, kv, vv, dov = q_ref[...], k_ref[0:nk, :], v_ref[0:nk, :], do_ref[...]
            p = _softmax_rows(_dot(qv, kv, 'nt') * scale)
            dv_ref[0:nk, :] += _dot(p, dov, 'tn')
            dp = _dot(dov, vv, 'nt')
            ds = p * (dp - jnp.sum(dp * p, -1, keepdims=True)) * scale
            dq_ref[...] = _dot(ds, kv, 'nn')
            dk_ref[0:nk, :] += _dot(ds, qv, 'tn')

        pl.when(pl.program_id(2) == 0)(lambda: run(n_ctx))
        pl.when(pl.program_id(2) > 0)(lambda: run(n))

    return pl.pallas_call(
        body, name='attn_bwd', grid=(ATT_KVH, grp, n // tq),
        in_specs=[pl.BlockSpec((tq, ATT_D), lambda h, g, i: (i, h * grp + g)),
                  pl.BlockSpec((n, ATT_D), lambda h, g, i: (0, h)),
                  pl.BlockSpec((n, ATT_D), lambda h, g, i: (0, h)),
                  pl.BlockSpec((tq, ATT_D), lambda h, g, i: (i, h * grp + g))],
        out_specs=[pl.BlockSpec((tq, ATT_D), lambda h, g, i: (i, h * grp + g)),
                   pl.BlockSpec((n, ATT_D), lambda h, g, i: (0, h)),
                   pl.BlockSpec((n, ATT_D), lambda h, g, i: (0, h))],
        out_shape=[jax.ShapeDtypeStruct((n, ATT_Q), F32), jax.ShapeDtypeStruct((n, ATT_KV), F32),
                   jax.ShapeDtypeStruct((n, ATT_KV), F32)],
        compiler_params=_params(("parallel", "arbitrary", "arbitrary")),
    )(q, k, v, do)


def make_attention(n_ctx):
    @jax.custom_vjp
    def attention(q, k, v):
        return attention_fwd_call(q, k, v, n_ctx)

    def fwd(q, k, v):
        return attention_fwd_call(q, k, v, n_ctx), (q, k, v)

    def bwd(res, do):
        return tuple(attention_bwd_call(*res, do, n_ctx))

    attention.defvjp(fwd, bwd)
    return attention


def _log_sigmoid(z):
    return jnp.minimum(z, 0.0) - jnp.log(1.0 + jnp.exp(-jnp.abs(z)))


def _gla_chunk(d, q, k, v, glr, wg, bg, state, mm, tri_dot):
    chunk = q.shape[0]
    r = lax.broadcasted_iota(jnp.int32, (chunk, chunk), 0)
    c = lax.broadcasted_iota(jnp.int32, (chunk, chunk), 1)
    lower = jnp.where(c <= r, 1.0, 0.0).astype(F32)
    upper = jnp.where(c >= r, 1.0, 0.0).astype(F32)
    tri = jnp.where(d == 0, lower, upper)
    tri_t = jnp.where(d == 0, upper, lower)
    log_a = _log_sigmoid(mm(glr, wg, 'nn') + bg) * (1.0 / GLA_TAU)
    outs, new_state = [], []
    for h in range(GLA_HEADS):
        la = log_a[:, h * GLA_DK:(h + 1) * GLA_DK]
        b = tri_dot(tri, tri_t, la)
        tot = jnp.sum(la, axis=0, keepdims=True)
        qh = q[:, h * GLA_DK:(h + 1) * GLA_DK] * (GLA_DK ** -0.5)
        kh = k[:, h * GLA_DK:(h + 1) * GLA_DK]
        vh = v[:, h * GLA_DV:(h + 1) * GLA_DV]
        qe = qh * jnp.exp(b)
        ke = kh * jnp.exp(-b)
        att = mm(qe, ke, 'nt') * tri
        outs.append(mm(att, vh, 'nn') + mm(qe, state[h], 'nn'))
        kd = kh * jnp.exp(tot - b)
        decay = jnp.transpose(jnp.broadcast_to(jnp.exp(tot), (GLA_DK, GLA_DK)))
        decay = jnp.concatenate([decay] * (GLA_DV // GLA_DK), axis=1)
        new_state.append(decay * state[h] + mm(kd, vh, 'tn'))
    return jnp.concatenate(outs, axis=1), tuple(new_state)


def _gla_chunk_of(d, s, nc, ns):
    falling = jnp.where(s < nc, nc - 1 - s, ns - 1 - (s - nc))
    return jnp.where(d == 0, s, falling)


def gla_fwd_call(q, k, v, glr, wg, bg, n_ctx):
    n = q.shape[0]
    ch = GLA_CHUNK
    ns, nc = n // ch, n_ctx // ch

    def at(d, s):
        return _gla_chunk_of(d, s, nc, ns)

    def body(q_ref, k_ref, v_ref, glr_ref, wg_ref, bg_ref, o_ref, saved_ref, state_ref):
        d, s = pl.program_id(0), pl.program_id(1)

        @pl.when(s == 0)
        def _():
            state_ref[...] = jnp.zeros_like(state_ref)

        saved_ref[...] = state_ref[...]
        state = tuple(state_ref[h] for h in range(GLA_HEADS))
        o, new_state = _gla_chunk(d, q_ref[...], k_ref[...], v_ref[...], glr_ref[...], wg_ref[...],
                                  bg_ref[...], state, _dot, _tri_dot_plain)
        o_ref[...] = o
        for h in range(GLA_HEADS):
            state_ref[h] = new_state[h]

    return pl.pallas_call(
        body, name='gla_fwd', grid=(2, ns),
        in_specs=[pl.BlockSpec((ch, GLA_QK), lambda d, s: (at(d, s), 0)),
                  pl.BlockSpec((ch, GLA_QK), lambda d, s: (at(d, s), 0)),
                  pl.BlockSpec((ch, GLA_V), lambda d, s: (at(d, s), 0)),
                  pl.BlockSpec((ch, LANE), lambda d, s: (at(d, s), 0)),
                  pl.BlockSpec((None, LANE, GLA_QK), lambda d, s: (d, 0, 0)),
                  pl.BlockSpec((None, 1, GLA_QK), lambda d, s: (d, 0, 0))],
        out_specs=[pl.BlockSpec((None, ch, GLA_V), lambda d, s: (d, at(d, s), 0)),
                   pl.BlockSpec((None, None, GLA_HEADS, GLA_DK, GLA_DV), lambda d, s: (d, s, 0, 0, 0))],
        out_shape=[jax.ShapeDtypeStruct((2, n, GLA_V), F32),
                   jax.ShapeDtypeStruct((2, ns, GLA_HEADS, GLA_DK, GLA_DV), F32)],
        scratch_shapes=[pltpu.VMEM((GLA_HEADS, GLA_DK, GLA_DV), F32)],
        compiler_params=_params(("parallel", "arbitrary")),
    )(q, k, v, glr, wg, bg)


def gla_bwd_call(q, k, v, glr, wg, bg, saved, do, n_ctx):
    n = q.shape[0]
    ch = GLA_CHUNK
    ns, nc = n // ch, n_ctx // ch

    def at(d, s):
        return _gla_chunk_of(d, ns - 1 - s, nc, ns)

    def body(q_ref, k_ref, v_ref, glr_ref, wg_ref, bg_ref, saved_ref, do_ref,
             dq_ref, dk_ref, dv_ref, dglr_ref, dwg_ref, dbg_ref, dstate_ref):
        d, s = pl.program_id(0), pl.program_id(1)

        @pl.when(s == 0)
        def _():
            dstate_ref[...] = jnp.zeros_like(dstate_ref)
            dwg_ref[...] = jnp.zeros_like(dwg_ref)
            dbg_ref[...] = jnp.zeros_like(dbg_ref)

        state = tuple(saved_ref[h] for h in range(GLA_HEADS))
        dstate = tuple(dstate_ref[h] for h in range(GLA_HEADS))

        def f(qv, kv, vv, glrv, wgv, bgv, st):
            return _gla_chunk(d, qv, kv, vv, glrv, wgv, bgv, st, _mm, _tri_dot)

        _, vjp = jax.vjp(f, q_ref[...], k_ref[...], v_ref[...], glr_ref[...], wg_ref[...], bg_ref[...], state)
        dq, dk, dv, dglr, dwg, dbg, dprev = vjp((do_ref[...], dstate))
        dq_ref[...] = dq
        dk_ref[...] = dk
        dv_ref[...] = dv
        dglr_ref[...] = dglr
        dwg_ref[...] += dwg
        dbg_ref[...] += dbg
        for h in range(GLA_HEADS):
            dstate_ref[h] = dprev[h]

    return pl.pallas_call(
        body, name='gla_bwd', grid=(2, ns),
        in_specs=[pl.BlockSpec((ch, GLA_QK), lambda d, s: (at(d, s), 0)),
                  pl.BlockSpec((ch, GLA_QK), lambda d, s: (at(d, s), 0)),
                  pl.BlockSpec((ch, GLA_V), lambda d, s: (at(d, s), 0)),
                  pl.BlockSpec((ch, LANE), lambda d, s: (at(d, s), 0)),
                  pl.BlockSpec((None, LANE, GLA_QK), lambda d, s: (d, 0, 0)),
                  pl.BlockSpec((None, 1, GLA_QK), lambda d, s: (d, 0, 0)),
                  pl.BlockSpec((None, None, GLA_HEADS, GLA_DK, GLA_DV), lambda d, s: (d, ns - 1 - s, 0, 0, 0)),
                  pl.BlockSpec((None, ch, GLA_V), lambda d, s: (d, at(d, s), 0))],
        out_specs=[pl.BlockSpec((None, ch, GLA_QK), lambda d, s: (d, at(d, s), 0)),
                   pl.BlockSpec((None, ch, GLA_QK), lambda d, s: (d, at(d, s), 0)),
                   pl.BlockSpec((None, ch, GLA_V), lambda d, s: (d, at(d, s), 0)),
                   pl.BlockSpec((None, ch, LANE), lambda d, s: (d, at(d, s), 0)),
                   pl.BlockSpec((None, LANE, GLA_QK), lambda d, s: (d, 0, 0)),
                   pl.BlockSpec((None, 1, GLA_QK), lambda d, s: (d, 0, 0))],
        out_shape=[jax.ShapeDtypeStruct((2, n, GLA_QK), F32), jax.ShapeDtypeStruct((2, n, GLA_QK), F32),
                   jax.ShapeDtypeStruct((2, n, GLA_V), F32), jax.ShapeDtypeStruct((2, n, LANE), F32),
                   jax.ShapeDtypeStruct((2, LANE, GLA_QK), F32), jax.ShapeDtypeStruct((2, 1, GLA_QK), F32)],
        scratch_shapes=[pltpu.VMEM((GLA_HEADS, GLA_DK, GLA_DV), F32)],
        compiler_params=_params(("parallel", "arbitrary")),
    )(q, k, v, glr, wg, bg, saved, do)


def make_gla(n_ctx):
    @jax.custom_vjp
    def gla(q, k, v, glr, wg, bg):
        return gla_fwd_call(q, k, v, glr, wg, bg, n_ctx)[0]

    def fwd(q, k, v, glr, wg, bg):
        o, saved = gla_fwd_call(q, k, v, glr, wg, bg, n_ctx)
        return o, (q, k, v, glr, wg, bg, saved)

    def bwd(res, do):
        dq, dk, dv, dglr, dwg, dbg = gla_bwd_call(*res, do, n_ctx)
        return dq[0] + dq[1], dk[0] + dk[1], dv[0] + dv[1], dglr[0] + dglr[1], dwg, dbg

    gla.defvjp(fwd, bwd)
    return gla


def _s5_chunk_at(kind, g, nch):
    if kind == 0:
        return g
    if kind == 1:
        return jnp.where(g == 0, 0, nch - g)
    if kind == 2:
        return nch - 1 - g
    return jnp.where(g == nch - 1, 0, g + 1)


def s5_scan_call(xr, xi, ar, ai, kind, lc, name):
    nch = xr.shape[0] // lc
    rising = kind in (0, 3)

    def body(xr_ref, xi_ref, ar_ref, ai_ref, sr_ref, si_ref, st_ref):
        g = pl.program_id(0)

        @pl.when(g == 0)
        def _():
            st_ref[...] = jnp.zeros_like(st_ref)

        a_r, a_i = ar_ref[...], ai_ref[...]

        def step(j, carry):
            cr, ci = carry
            tt = j if rising else lc - 1 - j
            nr = a_r * cr - a_i * ci + xr_ref[tt]
            ni = a_r * ci + a_i * cr + xi_ref[tt]
            sr_ref[tt] = nr
            si_ref[tt] = ni
            return nr, ni

        cr, ci = lax.fori_loop(0, lc, step, (st_ref[0], st_ref[1]), unroll=8)
        st_ref[0] = cr
        st_ref[1] = ci

    blk = pl.BlockSpec((lc, S5_ROWS, LANE), lambda g: (_s5_chunk_at(kind, g, nch), 0, 0))
    par = pl.BlockSpec((S5_ROWS, LANE), lambda g: (0, 0))
    return pl.pallas_call(
        body, name=name, grid=(nch,), in_specs=[blk, blk, par, par], out_specs=[blk, blk],
        scratch_shapes=[pltpu.VMEM((2, S5_ROWS, LANE), F32)],
        out_shape=[jax.ShapeDtypeStruct(xr.shape, F32), jax.ShapeDtypeStruct(xr.shape, F32)],
        compiler_params=_params(("arbitrary",)),
    )(xr, xi, ar, ai)


def s5_da_call(lr, li, pr, pi, lc):
    t = lr.shape[0]

    def body(lr_ref, li_ref, pr_ref, pi_ref, dar_ref, dai_ref):
        @pl.when(pl.program_id(0) == 0)
        def _():
            dar_ref[...] = jnp.zeros_like(dar_ref)
            dai_ref[...] = jnp.zeros_like(dai_ref)

        a, b, c, e = lr_ref[...], li_ref[...], pr_ref[...], pi_ref[...]
        dar_ref[...] += jnp.sum(a * c + b * e, axis=0)
        dai_ref[...] += jnp.sum(b * c - a * e, axis=0)

    blk = pl.BlockSpec((lc, S5_ROWS, LANE), lambda g: (g, 0, 0))
    par = pl.BlockSpec((S5_ROWS, LANE), lambda g: (0, 0))
    return pl.pallas_call(
        body, name='s5_da', grid=(t // lc,), in_specs=[blk] * 4, out_specs=[par, par],
        out_shape=[jax.ShapeDtypeStruct((S5_ROWS, LANE), F32)] * 2,
        compiler_params=_params(("arbitrary",)),
    )(lr, li, pr, pi)


def make_s5_scan(direction, n_chunks, lc):
    def predecessors(s):
        zero = jnp.zeros((1,) + s.shape[1:], F32)
        if direction == 0:
            return jnp.concatenate([zero, s[:-1]])
        return jnp.concatenate([s[1:lc], zero, s[lc + 1:], s[0:1]])

    @jax.custom_vjp
    def scan(xr, xi, ar, ai):
        return tuple(s5_scan_call(xr, xi, ar, ai, direction, lc, 's5_scan_d%d' % direction))

    def fwd(xr, xi, ar, ai):
        sr, si = s5_scan_call(xr, xi, ar, ai, direction, lc, 's5_scan_d%d' % direction)
        return (sr, si), (sr, si, ar, ai)

    def bwd(res, g):
        sr, si, ar, ai = res
        lr, li = s5_scan_call(g[0], g[1], ar, -ai, 2 + direction, lc, 's5_adjoint_d%d' % direction)
        dar, dai = s5_da_call(lr, li, predecessors(sr), predecessors(si), lc)
        return lr, li, dar, dai

    scan.defvjp(fwd, bwd)
    return scan


def loss_fwd_call(y, target, tile):
    n, dm = y.shape

    def body(y_ref, t_ref, o_ref):
        e = y_ref[...] - t_ref[...]
        o_ref[...] = jnp.full(o_ref.shape, 0.5 * jnp.sum(e * e) / dm, F32)

    out = pl.pallas_call(
        body, name='loss_fwd', grid=(n // tile,),
        in_specs=[pl.BlockSpec((tile, dm), lambda i: (i, 0))] * 2,
        out_specs=pl.BlockSpec((8, LANE), lambda i: (i, 0)),
        out_shape=jax.ShapeDtypeStruct((8 * (n // tile), LANE), F32),
        compiler_params=_params(("parallel",)),
    )(y, target)
    return jnp.sum(out[::8, 0])


def make_loss(tile):
    @jax.custom_vjp
    def loss(y, target):
        return loss_fwd_call(y, target, tile)

    def fwd(y, target):
        return loss_fwd_call(y, target, tile), (y, target)

    def bwd(res, g):
        y, target = res
        scale = jnp.full((1, y.shape[1]), g / y.shape[1], F32)

        def body(y_ref, t_ref, s_ref, o_ref):
            o_ref[...] = (y_ref[...] - t_ref[...]) * s_ref[...]

        dy = pl.pallas_call(
            body, name='loss_bwd', grid=(y.shape[0] // tile,),
            in_specs=[pl.BlockSpec((tile, y.shape[1]), lambda i: (i, 0))] * 2 +
                     [pl.BlockSpec((1, y.shape[1]), lambda i: (0, 0))],
            out_specs=pl.BlockSpec((tile, y.shape[1]), lambda i: (i, 0)),
            out_shape=jax.ShapeDtypeStruct(y.shape, F32),
            compiler_params=_params(("parallel",)),
        )(y, target, scale)
        return dy, jnp.zeros_like(target)

    loss.defvjp(fwd, bwd)
    return loss


def all_gather(x, name):
    def body(x_ref, out_ref, send_sems, recv_sems, local_sem):
        x, y, c = _coords()
        me, sibling = (x, y, c), (x, y, 1 - c)
        chips = [(1 - x, y), (x, 1 - y), (1 - x, 1 - y)]

        def slot(px, py, pc):
            return out_ref.at[4 * px + 2 * py + pc]

        def copy(k, block, to, src=None):
            return pltpu.make_async_remote_copy(
                src_ref=slot(*block) if src is None else src, dst_ref=slot(*block),
                send_sem=send_sems.at[k], recv_sem=recv_sems.at[k], device_id=to, device_id_type=MESH)

        mine = pltpu.make_async_copy(x_ref, slot(*me), local_sem)
        mine.start()
        first = [copy(0, me, sibling, src=x_ref)]
        first += [copy(1 + j, me, (*chip, c), src=x_ref) for j, chip in enumerate(chips)]
        for cp in first:
            cp.start()
        passed = [copy(4 + j, (*chip, c), sibling) for j, chip in enumerate(chips)]
        for j, chip in enumerate(chips):
            copy(1 + j, (*chip, c), me).wait_recv()
            passed[j].start()
        copy(0, sibling, me).wait_recv()
        for j, chip in enumerate(chips):
            copy(4 + j, (*chip, 1 - c), me).wait_recv()
        for cp in first + passed:
            cp.wait_send()
        mine.wait()

    return pl.pallas_call(
        body, name=name, out_shape=jax.ShapeDtypeStruct((N_DEV,) + x.shape, x.dtype),
        in_specs=[pl.BlockSpec(memory_space=pl.ANY)], out_specs=pl.BlockSpec(memory_space=pl.ANY),
        scratch_shapes=[pltpu.SemaphoreType.DMA((7,)), pltpu.SemaphoreType.DMA((7,)), pltpu.SemaphoreType.DMA],
    )(x)


def all_to_all(x, name):
    def body(x_ref, out_ref, send_sems, recv_sems, local_sem):
        refs = ('scatter', x_ref, out_ref, send_sems, recv_sems, local_sem)
        _exchange_start(*refs)
        _exchange_wait(*refs)

    return pl.pallas_call(
        body, name=name, out_shape=jax.ShapeDtypeStruct(x.shape, x.dtype),
        in_specs=[pl.BlockSpec(memory_space=pl.ANY)], out_specs=pl.BlockSpec(memory_space=pl.ANY),
        scratch_shapes=_EXCHANGE_SEMS,
    )(x)


def sum_parts_call(parts, name):
    p, r, c = parts.shape
    tr = _pick(r, 256, 8)

    def body(p_ref, o_ref):
        acc = p_ref[0].astype(F32)
        for j in range(1, p):
            acc = acc + p_ref[j].astype(F32)
        o_ref[...] = acc

    return pl.pallas_call(
        body, name=name, grid=(r // tr,), in_specs=[pl.BlockSpec((p, tr, c), lambda i: (0, i, 0))],
        out_specs=pl.BlockSpec((tr, c), lambda i: (i, 0)), out_shape=jax.ShapeDtypeStruct((r, c), F32),
        compiler_params=_params(("parallel",)),
    )(parts)


@jax.custom_vjp
def gather_rows(x):
    return all_gather(x, 'mod_gather')


def _gather_rows_fwd(x):
    return all_gather(x, 'mod_gather'), None


def _gather_rows_bwd(_, ct):
    return (sum_parts_call(all_to_all(ct, 'mod_scatter'), 'mod_scatter_sum'),)


gather_rows.defvjp(_gather_rows_fwd, _gather_rows_bwd)


def adamw_call(parts, w, m, v, name):
    p, r, c = parts.shape
    tr = _pick(r, 128, 8)
    c1 = 1.0 / (1.0 - ADAM_B1 ** ADAM_STEP)
    c2 = 1.0 / (1.0 - ADAM_B2 ** ADAM_STEP)

    def body(p_ref, w_ref, m_ref, v_ref, g_ref, d_ref, m2_ref, v2_ref):
        g = p_ref[0].astype(F32)
        for j in range(1, p):
            g = g + p_ref[j].astype(F32)
        m2 = ADAM_B1 * m_ref[...] + (1.0 - ADAM_B1) * g
        v2 = ADAM_B2 * v_ref[...] + (1.0 - ADAM_B2) * (g * g)
        g_ref[...] = g
        m2_ref[...] = m2
        v2_ref[...] = v2
        d_ref[...] = -ADAM_LR * ((m2 * c1) / (jnp.sqrt(v2 * c2) + ADAM_EPS) + ADAM_WD * w_ref[...])

    blk = pl.BlockSpec((tr, c), lambda i: (i, 0))
    return pl.pallas_call(
        body, name=name, grid=(r // tr,),
        in_specs=[pl.BlockSpec((p, tr, c), lambda i: (0, i, 0)), blk, blk, blk], out_specs=[blk] * 4,
        out_shape=[jax.ShapeDtypeStruct((r, c), F32)] * 4,
        compiler_params=_params(("parallel",)),
    )(parts, w, m, v)


def _deinterleave(z, heads):
    lead = z.shape[:-1]
    return z.reshape(lead + (heads, ATT_D // 2, 2)).swapaxes(-1, -2).reshape(lead + (heads * ATT_D,))


def _unshard_cols(g):
    return jnp.moveaxis(g, 0, -2).reshape(g.shape[1:-1] + (N_DEV * g.shape[-1],))


def _pack_w_in(g, dm):
    w = _unshard_cols(g)
    splits = (GLA_QK, GLA_QK, GLA_V, GLA_V, GLA_RANK, S5_WIDTH, ATT_Q, ATT_KV, ATT_KV, 3 * dm)
    gq, gk, gv, gr, glr, su, aq, ak, av, bg = jnp.split(w, np.cumsum(splits)[:-1].tolist(), axis=-1)
    pad = jnp.zeros(w.shape[:-1] + (LANE - GLA_RANK,), w.dtype)
    return jnp.concatenate([gq, gk, gv, gr, su, _deinterleave(aq, ATT_QH), _deinterleave(ak, ATT_KVH),
                            av, bg, glr, pad], axis=-1)


def _in_widths(dm):
    return (GLA_QK, GLA_QK, GLA_V, GLA_V, S5_WIDTH, ATT_Q, ATT_KV, ATT_KV, 3 * dm, LANE)


def _make_split(widths):
    cuts = np.cumsum(widths)[:-1].tolist()

    @jax.custom_vjp
    def split(z):
        return tuple(jnp.split(z, cuts, axis=1))

    def fwd(z):
        return tuple(jnp.split(z, cuts, axis=1)), None

    def bwd(_, cots):
        return (jnp.concatenate(cots, axis=1),)

    split.defvjp(fwd, bwd)
    return split


def _block_diag_in(b):
    eye = jnp.eye(S5_GROUPS, dtype=F32)
    return jnp.einsum('gpc,gh->gchp', b, eye).reshape(S5_WIDTH, S5_CH)


def _block_diag_out(cm):
    eye = jnp.eye(S5_GROUPS, dtype=F32)
    return jnp.einsum('gcp,gh->gphc', cm, eye).reshape(S5_CH, S5_WIDTH)


def _s5_discretise(lam_re, lam_im, log_dt):
    dt = jnp.exp(log_dt)[:, None]
    mag = jnp.exp(lam_re * dt)
    a_re, a_im = mag * jnp.cos(lam_im * dt), mag * jnp.sin(lam_im * dt)
    den = lam_re * lam_re + lam_im * lam_im
    nr, ni = a_re - 1, a_im
    k_re = (nr * lam_re + ni * lam_im) / den
    k_im = (ni * lam_re - nr * lam_im) / den
    return a_re, a_im, k_re, k_im


def _rope_tables(n_ctx, n_lat):
    rows = jnp.repeat(jnp.arange(n_lat // GRID_W), GRID_W).astype(F32)
    cols = jnp.tile(jnp.arange(GRID_W), n_lat // GRID_W).astype(F32)
    n_freq = ATT_D // 4
    inv = ROPE_THETA ** (-jnp.arange(n_freq, dtype=F32) / n_freq)
    ang = jnp.concatenate([rows[:, None] * inv, cols[:, None] * inv], -1)
    cos, sin = jnp.cos(ang), jnp.sin(ang)
    cosf = jnp.concatenate([jnp.ones((n_ctx, ATT_D), F32), jnp.concatenate([cos, cos], -1)], 0)
    sinf = jnp.concatenate([jnp.zeros((n_ctx, ATT_D), F32), jnp.concatenate([-sin, sin], -1)], 0)
    return cosf, sinf


def _build_ops(n, n_ctx, dm, dff, depth):
    t256 = min(256, n_ctx)
    t64 = 64
    alpha = (2 * depth) ** 0.25
    ops = {}

    def seg(i, tile, pc, pl_):
        return jnp.where(i < n_ctx // tile, pc, pl_)

    def modulate_f(i, rows, params):
        sh = seg(i, t256, params[0], params[1])
        sc = seg(i, t256, params[2], params[3])
        return (rows[0] * (1.0 + sc) + sh,)

    def postnorm_f(i, rows, params):
        g = seg(i, t256, params[0], params[1])
        return (_layer_norm(alpha * rows[0] + g * rows[1], params[2], params[3]),)

    def gla_out_f(i, rows, params):
        o = rows[0] + rows[1]
        heads = [_layer_norm(o[:, h * GLA_DV:(h + 1) * GLA_DV], None, None) for h in range(GLA_HEADS)]
        return (jnp.concatenate(heads, axis=1) * params[0] * _silu(rows[2]),)

    def s5_in_f(i, rows, params):
        br, bi = rows[0][:, :S5_CH], rows[0][:, S5_CH:]
        kr0, ki0, kr1, ki1 = params
        return (kr0 * br - ki0 * bi, kr0 * bi + ki0 * br, kr1 * br - ki1 * bi, kr1 * bi + ki1 * br)

    def s5_mid_f(i, rows, params):
        return (jax.nn.gelu(rows[0] + rows[1] * params[0]),)

    def s5_gate_f(i, rows, params):
        return (rows[0] * jax.nn.sigmoid(rows[1]),)

    def qk_f(swap):
        def f(i, rows, params):
            aq, ak, cosf, sinf = rows
            outs = []
            for z, w, heads in ((aq, params[0], ATT_QH), (ak, params[1], ATT_KVH)):
                hs = []
                for h in range(heads):
                    zh = z[:, h * ATT_D:(h + 1) * ATT_D]
                    zn = zh * lax.rsqrt(jnp.mean(zh * zh, -1, keepdims=True) + EPS) * w
                    hs.append(zn * cosf + swap(zn) * sinf)
                outs.append(jnp.concatenate(hs, axis=1))
            return tuple(outs)
        return f

    def silu_f(i, rows, params):
        return (_silu(rows[0]),)

    def merge_f(i, rows, params):
        pg, ps, pa, bg = rows
        gate = jax.nn.sigmoid(bg)
        return (gate[:, :dm] * pg + gate[:, dm:2 * dm] * ps + gate[:, 2 * dm:] * pa,)

    def swiglu_f(i, rows, params):
        return (_silu(rows[0]) * rows[1],)

    ops['modulate1'] = make_rowwise('modulate1', modulate_f, (dm,), t256)
    ops['modulate2'] = make_rowwise('modulate2', modulate_f, (dm,), t256)
    ops['postnorm1'] = make_rowwise('postnorm1', postnorm_f, (dm,), t256)
    ops['postnorm2'] = make_rowwise('postnorm2', postnorm_f, (dm,), t256)
    ops['gla_out'] = make_rowwise('gla_out', gla_out_f, (GLA_V,), t256)
    ops['s5_in'] = make_rowwise('s5_in', s5_in_f, (S5_CH,) * 4, t64)
    ops['s5_mid'] = make_rowwise('s5_mid', s5_mid_f, (S5_WIDTH,), t256)
    ops['s5_gate'] = make_rowwise('s5_gate', s5_gate_f, (S5_WIDTH,), t256)
    ops['merge'] = make_rowwise('merge', merge_f, (dm,), t64)
    ops['swiglu'] = make_rowwise('swiglu', swiglu_f, (dff,), t64)
    ops['qk_prep'] = make_rowwise('qk_prep', qk_f(_swap_plain), (ATT_Q, ATT_KV), t256, n_nondiff=2,
                                  f_diff=qk_f(_swap_halves))
    ops['silu_c'] = make_rowwise('silu_c', silu_f, (dm,), 16)
    return ops


def _flat2d(a):
    return a.reshape(-1, a.shape[-1])


def _step(given):
    x, c, ctx = given['x'][0], given['c'], given['ctx'][0]
    target = given['loss_target'][0]
    n_lat, dm = x.shape
    n_ctx = ctx.shape[0]
    n = n_ctx + n_lat
    depth = given['w_ada'].shape[0]
    dff = given['w_ffn_out'].shape[1] * N_DEV
    ada_cols = given['w_ada'].shape[2]
    xi, yi, ci = _coords()
    me = 4 * xi + 2 * yi + ci
    lc = n_ctx
    ops = _build_ops(n, n_ctx, dm, dff, depth)
    names = ['ada', 'in', 's5b', 's5c', 'glu', 'pg', 'ps', 'pa', 'out', 'fa', 'fb', 'fo']
    lin = {k: make_linear('lin_' + k) for k in names}
    split_in = _make_split(_in_widths(dm))
    gla = make_gla(n_ctx)
    attention = make_attention(n_ctx)
    scans = [make_s5_scan(d, n // lc, lc) for d in range(2)]
    loss_op = make_loss(min(256, n_lat))
    cosf, sinf = _rope_tables(n_ctx, n_lat)

    carrier = {k: make_carrier_linear('lin_' + k) for k in ('in', 'fa', 'fo', 'out')}
    carried_by = {'w_in': 'in', 'w_ffn_in': 'fa', 'w_ffn_out': 'fo', 'misc': 'out'}

    def misc_block(src, prefix, l, dtype):
        return jnp.concatenate([src[prefix + k][l].astype(dtype).reshape(-1) for k in MISC]).reshape(-1, LANE)

    blocks = [{'w_in': given['w_in'][l].astype(BF16), 'w_ffn_in': given['w_ffn_in'][l].astype(BF16),
               'w_ffn_out': given['w_ffn_out'][l].astype(BF16), 'misc': misc_block(given, '', l, BF16)}
              for l in range(depth)]
    gathered0 = {k: all_gather(blocks[0][k], 'gather_' + k) for k in FAMILIES}
    gates = {k: all_gather(given[k], 'gather_' + k) for k in ('w_gla_gate', 'b_gla_gate')}
    c_all = all_gather(c, 'gather_c').reshape(N_DEV, dm)

    diff0 = {
        'x': x,
        'rep': {k: given[k] for k in REPLICATED},
        'grads': [{k: jnp.zeros((N_DEV,) + blocks[l][k].shape, BF16) for k in FAMILIES} for l in range(depth)],
        'gates': {k: jnp.zeros((N_DEV,) + given[k].shape, F32) for k in gates},
        'ada': jnp.zeros(given['w_ada'].shape, F32),
    }
    w_ada_bf = given['w_ada'].astype(BF16)
    misc_shapes = [given[k].shape[1:] for k in MISC]
    misc_sizes = [int(np.prod(s)) for s in misc_shapes]

    def unpack(g):
        parts = jnp.split(g['misc'].reshape(N_DEV, -1), np.cumsum(misc_sizes)[:-1].tolist(), axis=1)
        m = {k: p.reshape((N_DEV,) + s) for k, p, s in zip(MISC, parts, misc_shapes)}
        w_fi = _unshard_cols(g['w_ffn_in'])
        rows = lambda z: z.reshape((N_DEV * z.shape[1],) + z.shape[2:])
        return dict(w_in=_pack_w_in(g['w_in'], dm), w_fa=w_fi[:, :dff], w_fb=w_fi[:, dff:],
                    w_fo=rows(g['w_ffn_out']), w_out=rows(m['w_out']), w_pg=_unshard_cols(m['w_proj_gla']),
                    w_ps=_unshard_cols(m['w_proj_s5']), w_pa=_unshard_cols(m['w_proj_attn']),
                    w_glu=rows(m['w_s5_glu']))

    def loss_fn(diff):
        r = diff['rep']
        wgate = _unshard_cols(gates['w_gla_gate'] + diff['gates']['w_gla_gate'])
        wgate = jnp.pad(wgate, ((0, 0), (0, 0), (0, LANE - GLA_RANK), (0, 0)))
        bgate = _unshard_cols(gates['b_gla_gate'] + diff['gates']['b_gla_gate'])[:, :, None, :]

        disc = jax.vmap(jax.vmap(_s5_discretise))(r['s5_lam_re'], r['s5_lam_im'], r['s5_log_dt'])
        a_re, a_im = (z.reshape(depth, 2, S5_ROWS, LANE) for z in disc[:2])
        k_re, k_im = (z.reshape(depth, 2, 1, S5_CH) for z in disc[2:])
        b_full = jnp.concatenate([jax.vmap(_block_diag_in)(r['s5_b_re']), jax.vmap(_block_diag_in)(r['s5_b_im'])], -1)
        bd_out = jax.vmap(_block_diag_out)
        c_full = jnp.concatenate([bd_out(r['s5_c_re'][:, 0]), -bd_out(r['s5_c_im'][:, 0]),
                                  bd_out(r['s5_c_re'][:, 1]), -bd_out(r['s5_c_im'][:, 1])], 1)
        row = lambda z: z[:, None, :]
        qw, kw = row(_deinterleave(r['q_norm_w'], 1)), row(_deinterleave(r['k_norm_w'], 1))

        c16 = jnp.concatenate([c_all, r['c_ctx'][None], jnp.zeros((7, dm), F32)], 0)
        c16s, = ops['silu_c']((c16,), ())

        small = dict(w_ada=w_ada_bf, p_ada=diff['ada'], b_ada=r['b_ada'], wgate=wgate, bgate=bgate,
                     a_re=a_re, a_im=a_im, k_re=k_re, k_im=k_im, b_full=b_full, c_full=c_full, qw=qw, kw=kw,
                     gnw=row(r['gla_norm_w']), s5_d=row(r['s5_d']), ln1_w=row(r['ln1_w']), ln1_b=row(r['ln1_b']),
                     ln2_w=row(r['ln2_w']), ln2_b=row(r['ln2_b']))
        family_of = {v: f for f, v in carried_by.items()}

        def matmul(k, a, p, nxt, got):
            if nxt is None or k not in family_of:
                return lin[k](a, p['w_' + k], p['p_' + k])
            fam = family_of[k]
            out, got['w'][fam], got['g'][fam] = carrier[k](a, p['w_' + k], p['p_' + k], nxt[0][fam], nxt[1][fam])
            return out

        def layer(xc, p, nxt, got):
            msh = lin['ada'](c16s, p['w_ada'], p['p_ada']) + lax.dynamic_slice(p['b_ada'], (me * ada_cols,), (ada_cols,))
            mods = gather_rows(msh)
            m_lat = lax.dynamic_index_in_dim(mods, me, axis=1, keepdims=False).reshape(1, 6 * dm)
            m_ctx = mods[:, N_DEV, :].reshape(1, 6 * dm)
            sh1, sc1, g1, sh2, sc2, g2 = [(m_ctx[:, j * dm:(j + 1) * dm], m_lat[:, j * dm:(j + 1) * dm]) for j in range(6)]

            h, = ops['modulate1']((xc,), sh1 + sc1)
            gq, gk, gv, gr, su, aq, ak, av, bgt, glr = split_in(matmul('in', h, p, nxt, got))
            o2 = gla(gq, gk, gv, glr, p['wgate'], p['bgate'])
            o_gla, = ops['gla_out']((o2[0], o2[1], gr), (p['gnw'],))

            bu = lin['s5b'](su, p['b_full'].astype(BF16), p['b_full'])
            xin = ops['s5_in']((bu,), (p['k_re'][0], p['k_im'][0], p['k_re'][1], p['k_im'][1]))
            tile3 = lambda z: z.reshape(n, S5_ROWS, LANE)
            st = []
            for d in range(2):
                st += scans[d](tile3(xin[2 * d]), tile3(xin[2 * d + 1]), p['a_re'][d], p['a_im'][d])
            states = jnp.concatenate([z.reshape(n, S5_CH) for z in st], axis=1)
            y0 = lin['s5c'](states, p['c_full'].astype(BF16), p['c_full'])
            ya, = ops['s5_mid']((y0, su), (p['s5_d'],))
            o_s5, = ops['s5_gate']((ya, lin['glu'](ya, p['w_glu'], p['p_glu'])), ())

            q, k = ops['qk_prep']((aq, ak, cosf, sinf), (p['qw'], p['kw']))
            o_att = attention(q, k, av)

            merged, = ops['merge']((lin['pg'](o_gla, p['w_pg'], p['p_pg']), lin['ps'](o_s5, p['w_ps'], p['p_ps']),
                                    lin['pa'](o_att, p['w_pa'], p['p_pa']), bgt), ())
            mix = matmul('out', merged, p, nxt, got)
            x1, = ops['postnorm1']((xc, mix), g1 + (p['ln1_w'], p['ln1_b']))
            h2, = ops['modulate2']((x1,), sh2 + sc2)
            act, = ops['swiglu']((matmul('fa', h2, p, nxt, got), lin['fb'](h2, p['w_fb'], p['p_fb'])), ())
            x2, = ops['postnorm2']((x1, matmul('fo', act, p, nxt, got)), g2 + (p['ln2_w'], p['ln2_b']))
            return x2

        xc = jnp.concatenate([ctx, diff['x']], 0)
        weights, stand_ins = gathered0, diff['grads'][0]
        for l in range(depth):
            p = {k: v[l] for k, v in small.items()}
            p.update(unpack(weights))
            p.update({'p' + k[1:]: v for k, v in unpack({k: v.astype(F32) for k, v in stand_ins.items()}).items()})
            nxt = (blocks[l + 1], diff['grads'][l + 1]) if l + 1 < depth else None
            got = {'w': {}, 'g': {}}
            xc = layer(xc, p, nxt, got)
            weights, stand_ins = {k: lax.stop_gradient(v) for k, v in got['w'].items()}, got['g']
        return loss_op(xc[n_ctx:], target)

    loss_local, grads = jax.value_and_grad(loss_fn)(diff0)
    loss = lax.psum(loss_local, ("x", "y", "c"))

    out = {}

    def update(k, parts):
        w2 = _flat2d(given[k])
        res = adamw_call(parts.reshape((parts.shape[0],) + w2.shape), w2, _flat2d(given['m_' + k]),
                         _flat2d(given['v_' + k]), 'adamw_' + k)
        out[k] = [z.reshape(given[k].shape) for z in res]

    parts = {k: jnp.stack([all_to_all(grads['grads'][0][k], 'scatter_' + k)] +
                          [grads['grads'][l][k] for l in range(1, depth)], axis=1) for k in FAMILIES}
    for k in ('w_in', 'w_ffn_in', 'w_ffn_out'):
        update(k, parts[k])
    stacked = lambda prefix: jnp.concatenate([misc_block(given, prefix, l, F32) for l in range(depth)])
    res = adamw_call(parts['misc'].reshape(N_DEV, -1, LANE), stacked(''), stacked('m_'), stacked('v_'), 'adamw_misc')
    offs = np.cumsum([0] + misc_sizes)
    for j, k in enumerate(MISC):
        out[k] = [z.reshape(depth, -1)[:, offs[j]:offs[j + 1]].reshape(given[k].shape) for z in res]
    for k in gates:
        update(k, all_to_all(grads['gates'][k], 'scatter_' + k))
    update('w_ada', grads['ada'][None])

    sizes = [int(np.prod(given[k].shape)) for k in REPLICATED]
    total = sum(sizes)
    wide = 8 * LANE
    padded = -(-total // (8 * wide)) * (8 * wide)

    def flat(prefix, src):
        v = jnp.concatenate([src[prefix + k].reshape(-1) for k in REPLICATED] + [jnp.zeros((padded - total,), F32)])
        return v.reshape(padded // wide, wide)

    g_all = all_gather(flat('', grads['rep']), 'gather_small_grads')
    res = adamw_call(g_all, flat('', given), flat('m_', given), flat('v_', given), 'adamw_small')
    offs = np.cumsum([0] + sizes)
    for j, k in enumerate(REPLICATED):
        out[k] = [z.reshape(-1)[offs[j]:offs[j + 1]].reshape(given[k].shape) for z in res]

    grad_x = grads['x'][None]
    return (loss, grad_x) + tuple(out[k][j] for j in range(4) for k in WEIGHTS)


def kernel(x, c, ctx, c_ctx, w_ada, b_ada, w_in, w_gla_gate, b_gla_gate, gla_norm_w, s5_lam_re, s5_lam_im, s5_log_dt, s5_b_re, s5_b_im, s5_c_re, s5_c_im, s5_d, w_s5_glu, q_norm_w, k_norm_w, w_proj_gla, w_proj_s5, w_proj_attn, w_out, ln1_w, ln1_b, ln2_w, ln2_b, w_ffn_in, w_ffn_out, loss_target, m_c_ctx, m_w_ada, m_b_ada, m_w_in, m_w_gla_gate, m_b_gla_gate, m_gla_norm_w, m_s5_lam_re, m_s5_lam_im, m_s5_log_dt, m_s5_b_re, m_s5_b_im, m_s5_c_re, m_s5_c_im, m_s5_d, m_w_s5_glu, m_q_norm_w, m_k_norm_w, m_w_proj_gla, m_w_proj_s5, m_w_proj_attn, m_w_out, m_ln1_w, m_ln1_b, m_ln2_w, m_ln2_b, m_w_ffn_in, m_w_ffn_out, v_c_ctx, v_w_ada, v_b_ada, v_w_in, v_w_gla_gate, v_b_gla_gate, v_gla_norm_w, v_s5_lam_re, v_s5_lam_im, v_s5_log_dt, v_s5_b_re, v_s5_b_im, v_s5_c_re, v_s5_c_im, v_s5_d, v_w_s5_glu, v_q_norm_w, v_k_norm_w, v_w_proj_gla, v_w_proj_s5, v_w_proj_attn, v_w_out, v_ln1_w, v_ln1_b, v_ln2_w, v_ln2_b, v_w_ffn_in, v_w_ffn_out):
    return _step(dict(locals()))
```

```python
import functools
import math

import numpy as np
import jax
import jax.numpy as jnp
from jax import lax
from jax.experimental import pallas as pl
from jax.experimental.pallas import tpu as pltpu

F32 = jnp.float32
BF16 = jnp.bfloat16
MESH = pl.DeviceIdType.MESH
N_DEV = 8
VMEM_LIMIT_V7X = 48 * 1024 * 1024
LANE = 128
SINGLE_STEP_K = 2304

GRID_W = 64
GLA_HEADS, GLA_DK, GLA_DV = 4, 128, 256
GLA_QK, GLA_V = GLA_HEADS * GLA_DK, GLA_HEADS * GLA_DV
GLA_RANK, GLA_TAU, GLA_CHUNK = 16, 16.0, 64
S5_WIDTH, S5_GROUP, S5_GROUPS, S5_STATE = 768, 16, 48, 64
S5_CH = S5_GROUPS * S5_STATE
S5_ROWS = S5_CH // LANE
ATT_QH, ATT_KVH, ATT_D = 8, 2, 128
ATT_Q, ATT_KV = ATT_QH * ATT_D, ATT_KVH * ATT_D
ROPE_THETA = 10000.0
EPS = 1e-6
ADAM_LR, ADAM_B1, ADAM_B2, ADAM_EPS, ADAM_WD, ADAM_STEP = 0.001, 0.9, 0.999, 1e-08, 0.01, 10

WEIGHTS = ['c_ctx', 'w_ada', 'b_ada', 'w_in', 'w_gla_gate', 'b_gla_gate', 'gla_norm_w', 's5_lam_re',
           's5_lam_im', 's5_log_dt', 's5_b_re', 's5_b_im', 's5_c_re', 's5_c_im', 's5_d', 'w_s5_glu',
           'q_norm_w', 'k_norm_w', 'w_proj_gla', 'w_proj_s5', 'w_proj_attn', 'w_out', 'ln1_w', 'ln1_b',
           'ln2_w', 'ln2_b', 'w_ffn_in', 'w_ffn_out']
GATHERED = ['w_in', 'w_gla_gate', 'b_gla_gate', 'w_s5_glu', 'w_proj_gla', 'w_proj_s5', 'w_proj_attn',
            'w_out', 'w_ffn_in', 'w_ffn_out']
REPLICATED = [n for n in WEIGHTS if n not in GATHERED and n != 'w_ada']
MISC = ['w_out', 'w_proj_gla', 'w_proj_s5', 'w_proj_attn', 'w_s5_glu']
FAMILIES = ['w_in', 'w_ffn_in', 'w_ffn_out', 'misc']


def _params(sem=None):
    return pltpu.CompilerParams(dimension_semantics=sem, vmem_limit_bytes=VMEM_LIMIT_V7X)


def _pick(dim, target, align):
    best = None
    t = align
    while t <= min(dim, target):
        if dim % t == 0:
            best = t
        t += align
    return dim if best is None else best


_CONTRACT = {'nn': ((1,), (0,)), 'nt': ((1,), (1,)), 'tn': ((0,), (0,))}


def _dot(a, b, dims):
    return lax.dot_general(a.astype(BF16), b.astype(BF16), (_CONTRACT[dims], ((), ())),
                           preferred_element_type=F32)


@functools.partial(jax.custom_vjp, nondiff_argnums=(2,))
def _mm(a, b, dims):
    return _dot(a, b, dims)


def _mm_fwd(a, b, dims):
    return _dot(a, b, dims), (a, b)


def _mm_bwd(dims, res, g):
    a, b = res
    if dims == 'nn':
        return _dot(g, b, 'nt'), _dot(a, g, 'tn')
    if dims == 'nt':
        return _dot(g, b, 'nn'), _dot(g, a, 'tn')
    return _dot(b, g, 'nt'), _dot(a, g, 'nn')


_mm.defvjp(_mm_fwd, _mm_bwd)


@jax.custom_vjp
def _tri_dot(tri, tri_t, z):
    return lax.dot_general(tri, z, (((1,), (0,)), ((), ())), precision=lax.Precision.HIGHEST,
                           preferred_element_type=F32)


def _tri_dot_fwd(tri, tri_t, z):
    return _tri_dot(tri, tri_t, z), (tri, tri_t)


def _tri_dot_bwd(res, g):
    tri, tri_t = res
    dz = lax.dot_general(tri_t, g, (((1,), (0,)), ((), ())), precision=lax.Precision.HIGHEST,
                         preferred_element_type=F32)
    return jnp.zeros_like(tri), jnp.zeros_like(tri_t), dz


_tri_dot.defvjp(_tri_dot_fwd, _tri_dot_bwd)


def _tri_dot_plain(tri, tri_t, z):
    return lax.dot_general(tri, z, (((1,), (0,)), ((), ())), precision=lax.Precision.HIGHEST,
                           preferred_element_type=F32)


@jax.custom_vjp
def _swap_halves(x):
    return pltpu.roll(x, LANE // 2, 1)


def _swap_fwd(x):
    return pltpu.roll(x, LANE // 2, 1), None


def _swap_bwd(_, g):
    return (pltpu.roll(g, LANE // 2, 1),)


_swap_halves.defvjp(_swap_fwd, _swap_bwd)


def _swap_plain(x):
    return pltpu.roll(x, LANE // 2, 1)


def _coords():
    return lax.axis_index("x"), lax.axis_index("y"), lax.axis_index("c")


def _exchange_out_shape(kind, x):
    if kind == 'gather':
        return (N_DEV,) + x.shape
    return (N_DEV // 2,) + x.shape[1:] if kind == 'pair' else x.shape


def _exchange_copies(kind, x_ref, out_ref, send_sems, recv_sems, local_sem):
    x, y, c = _coords()
    chip, me = 2 * x + y, 4 * x + 2 * y + c
    sibling = (x, y, 1 - c)
    others = [(1 - x, y), (x, 1 - y), (1 - x, 1 - y)]

    def remote(k, src, dst, to):
        return pltpu.make_async_remote_copy(src_ref=src, dst_ref=dst, send_sem=send_sems.at[k],
                                            recv_sem=recv_sems.at[k], device_id=to, device_id_type=MESH)

    if kind == 'gather':
        slot = lambda px, py, pc: out_ref.at[4 * px + 2 * py + pc]
        local = pltpu.make_async_copy(x_ref, out_ref.at[me], local_sem)
        first = [remote(0, x_ref, out_ref.at[me], sibling)]
        first += [remote(1 + j, x_ref, out_ref.at[me], (px, py, c)) for j, (px, py) in enumerate(others)]
        landed = [(remote(1 + j, x_ref, slot(px, py, c), sibling),
                   remote(4 + j, slot(px, py, c), slot(px, py, c), sibling)) for j, (px, py) in enumerate(others)]
        last = [remote(0, x_ref, slot(x, y, 1 - c), sibling)]
        last += [remote(4 + j, x_ref, slot(px, py, 1 - c), sibling) for j, (px, py) in enumerate(others)]
        return local, first, landed, last
    if kind == 'scatter':
        local = pltpu.make_async_copy(x_ref.at[me], out_ref.at[me], local_sem)
        first = []
        for k in range(1, N_DEV):
            px, py, pc = (1 - x if k & 4 else x), (1 - y if k & 2 else y), (1 - c if k & 1 else c)
            first.append(remote(k - 1, x_ref.at[4 * px + 2 * py + pc], out_ref.at[me], (px, py, pc)))
        return local, first, [], first
    if kind == 'pair':
        first = [remote(q, x_ref.at[2 * q + 1 - c], out_ref.at[q], sibling) for q in range(N_DEV // 2)]
        return None, first, [], first
    assert kind == 'quad', kind
    local = pltpu.make_async_copy(x_ref.at[chip], out_ref.at[chip], local_sem)
    first = [remote(j, x_ref.at[2 * px + py], out_ref.at[chip], (px, py, c)) for j, (px, py) in enumerate(others)]
    return local, first, [], first


def _exchange_start(*refs):
    local, first, _, _ = _exchange_copies(*refs)
    if local is not None:
        local.start()
    for cp in first:
        cp.start()


def _exchange_wait(*refs):
    local, first, landed, last = _exchange_copies(*refs)
    for arrival, onward in landed:
        arrival.wait_recv()
        onward.start()
    for cp in last:
        cp.wait_recv()
    for cp in first + [onward for _, onward in landed]:
        cp.wait_send()
    if local is not None:
        local.wait()


_EXCHANGE_SEMS = [pltpu.SemaphoreType.DMA((N_DEV - 1,)), pltpu.SemaphoreType.DMA((N_DEV - 1,)),
                  pltpu.SemaphoreType.DMA]


def pmatmul(a, b, dims, name, exchange=None):
    if dims == 'nn':
        (m, k), (k2, n) = a.shape, b.shape
    elif dims == 'nt':
        (m, k), (n, k2) = a.shape, b.shape
    else:
        (k, m), (k2, n) = a.shape, b.shape
    assert k == k2, (a.shape, b.shape, dims)
    single = k <= SINGLE_STEP_K
    tm = _pick(m, 768, LANE if dims == 'tn' else 8)
    tn = _pick(n, 512 if (single and dims == 'tn') else 1024, LANE)
    tk = k if single else _pick(k, 1024, LANE)
    nk = k // tk
    a_spec = pl.BlockSpec((tk, tm), lambda i, j, kk: (kk, i)) if dims == 'tn' else \
        pl.BlockSpec((tm, tk), lambda i, j, kk: (i, kk))
    b_spec = pl.BlockSpec((tn, tk), lambda i, j, kk: (j, kk)) if dims == 'nt' else \
        pl.BlockSpec((tk, tn), lambda i, j, kk: (kk, j))
    ni, nj = m // tm, n // tn
    acc = [] if single else [pltpu.VMEM((tm, tn), F32)]

    def matmul_step(a_ref, b_ref, o_ref, *acc_ref):
        if single:
            o_ref[...] = _dot(a_ref[...], b_ref[...], dims)
            return
        acc_ref, = acc_ref
        kk = pl.program_id(2)

        @pl.when(kk == 0)
        def _():
            acc_ref[...] = jnp.zeros_like(acc_ref)

        acc_ref[...] += _dot(a_ref[...], b_ref[...], dims)

        @pl.when(kk == nk - 1)
        def _():
            o_ref[...] = acc_ref[...]

    out_spec = pl.BlockSpec((tm, tn), lambda i, j, kk: (i, j))
    if exchange is None:
        return pl.pallas_call(
            matmul_step, name=name, grid=(ni, nj, nk),
            in_specs=[a_spec, b_spec], out_specs=out_spec, out_shape=jax.ShapeDtypeStruct((m, n), F32),
            scratch_shapes=acc, compiler_params=_params(("parallel", "parallel", "arbitrary")),
        )(a, b)

    kind, x = exchange

    def body(a_ref, b_ref, x_ref, o_ref, got_ref, *scratch):
        i, j, kk = pl.program_id(0), pl.program_id(1), pl.program_id(2)
        refs = (kind, x_ref, got_ref) + scratch[len(acc):]

        @pl.when(jnp.logical_and(jnp.logical_and(i == 0, j == 0), kk == 0))
        def _():
            _exchange_start(*refs)

        matmul_step(a_ref, b_ref, o_ref, *scratch[:len(acc)])

        @pl.when(jnp.logical_and(jnp.logical_and(i == ni - 1, j == nj - 1), kk == nk - 1))
        def _():
            _exchange_wait(*refs)

    any_spec = pl.BlockSpec(memory_space=pl.ANY)
    return pl.pallas_call(
        body, name=name, grid=(ni, nj, nk),
        in_specs=[a_spec, b_spec, any_spec], out_specs=[out_spec, any_spec],
        out_shape=[jax.ShapeDtypeStruct((m, n), F32),
                   jax.ShapeDtypeStruct(_exchange_out_shape(kind, x), x.dtype)],
        scratch_shapes=acc + _EXCHANGE_SEMS,
        compiler_params=_params(("arbitrary", "arbitrary", "arbitrary")),
    )(a, b, x)


def make_linear(name):
    @jax.custom_vjp
    def linear(a, w, w_proxy):
        return pmatmul(a, w, 'nn', name + '_fwd')

    def fwd(a, w, w_proxy):
        return pmatmul(a, w, 'nn', name + '_fwd'), (a, w)

    def bwd(res, g):
        a, w = res
        return pmatmul(g, w, 'nt', name + '_da'), jnp.zeros_like(w), pmatmul(a, g, 'tn', name + '_dw')

    linear.defvjp(fwd, bwd)
    return linear


def make_carrier_linear(name):
    def run(a, w, block, parts):
        out, gathered = pmatmul(a, w, 'nn', name + '_fwd', exchange=('gather', block))
        return out, gathered, jnp.zeros((N_DEV,) + parts.shape[1:], parts.dtype)

    @jax.custom_vjp
    def carrier(a, w, w_proxy, block, parts):
        return run(a, w, block, parts)

    def fwd(a, w, w_proxy, block, parts):
        return run(a, w, block, parts), (a, w, block)

    def bwd(res, cts):
        a, w, block = res
        g, _, ct = cts
        da, other_core = pmatmul(g, w, 'nt', name + '_da', exchange=('pair', ct))
        c = lax.axis_index("c")
        mine = lax.dynamic_index_in_dim(ct.reshape((N_DEV // 2, 2) + ct.shape[1:]), c, axis=1, keepdims=False)
        dw, parts = pmatmul(a, g, 'tn', name + '_dw',
                            exchange=('quad', add_halves_call(mine, other_core, name + '_add')))
        return da, jnp.zeros_like(w), dw, jnp.zeros_like(block), parts

    carrier.defvjp(fwd, bwd)
    return carrier


def make_rowwise(name, f, out_widths, tile, n_nondiff=0, f_diff=None):
    f_diff = f if f_diff is None else f_diff

    def fwd_call(rows, params):
        n = rows[0].shape[0]
        nr, npar = len(rows), len(params)

        def body(*refs):
            i = pl.program_id(0)
            outs = f(i, tuple(r[...] for r in refs[:nr]), tuple(p[...] for p in refs[nr:nr + npar]))
            for o_ref, val in zip(refs[nr + npar:], outs):
                o_ref[...] = val

        return pl.pallas_call(
            body, name=name + '_fwd', grid=(n // tile,),
            in_specs=[pl.BlockSpec((tile, r.shape[1]), lambda i: (i, 0)) for r in rows] +
                     [pl.BlockSpec(p.shape, lambda i: (0, 0)) for p in params],
            out_specs=[pl.BlockSpec((tile, w), lambda i: (i, 0)) for w in out_widths],
            out_shape=[jax.ShapeDtypeStruct((n, w), F32) for w in out_widths],
            compiler_params=_params(("parallel",)),
        )(*rows, *params)

    def bwd_call(rows, params, cots):
        n = rows[0].shape[0]
        nr, npar, nc = len(rows), len(params), len(cots)
        nd = nr - n_nondiff

        def body(*refs):
            i = pl.program_id(0)
            rv = tuple(r[...] for r in refs[:nr])
            pv = tuple(p[...] for p in refs[nr:nr + npar])
            cv = tuple(c[...] for c in refs[nr + npar:nr + npar + nc])
            out_refs = refs[nr + npar + nc:]
            _, vjp = jax.vjp(lambda dr, pp: tuple(f_diff(i, dr + rv[nd:], pp)), rv[:nd], pv)
            drows, dparams = vjp(cv)
            for o_ref, val in zip(out_refs[:nd], drows):
                o_ref[...] = val

            @pl.when(i == 0)
            def _():
                for o_ref in out_refs[nd:]:
                    o_ref[...] = jnp.zeros_like(o_ref)

            for o_ref, val in zip(out_refs[nd:], dparams):
                o_ref[...] += val

        outs = pl.pallas_call(
            body, name=name + '_bwd', grid=(n // tile,),
            in_specs=[pl.BlockSpec((tile, r.shape[1]), lambda i: (i, 0)) for r in rows] +
                     [pl.BlockSpec(p.shape, lambda i: (0, 0)) for p in params] +
                     [pl.BlockSpec((tile, c.shape[1]), lambda i: (i, 0)) for c in cots],
            out_specs=[pl.BlockSpec((tile, r.shape[1]), lambda i: (i, 0)) for r in rows[:nd]] +
                      [pl.BlockSpec(p.shape, lambda i: (0, 0)) for p in params],
            out_shape=[jax.ShapeDtypeStruct(r.shape, F32) for r in rows[:nd]] +
                      [jax.ShapeDtypeStruct(p.shape, F32) for p in params],
            compiler_params=_params(("arbitrary",)),
        )(*rows, *params, *cots)
        drows = tuple(outs[:nd]) + tuple(jnp.zeros_like(r) for r in rows[nd:])
        return drows, tuple(outs[nd:])

    @jax.custom_vjp
    def op(rows, params):
        return tuple(fwd_call(rows, params))

    def fwd(rows, params):
        return tuple(fwd_call(rows, params)), (rows, params)

    def bwd(res, cots):
        rows, params = res
        return bwd_call(rows, params, tuple(cots))

    op.defvjp(fwd, bwd)
    return op


def _silu(x):
    return x * jax.nn.sigmoid(x)


def _layer_norm(z, w, b):
    mu = jnp.mean(z, -1, keepdims=True)
    zc = z - mu
    var = jnp.mean(zc * zc, -1, keepdims=True)
    y = zc * lax.rsqrt(var + EPS)
    return y if w is None else y * w + b


def _softmax_rows(s):
    e = jnp.exp(s - jnp.max(s, -1, keepdims=True))
    return e / jnp.sum(e, -1, keepdims=True)


def attention_fwd_call(q, k, v, n_ctx):
    n = q.shape[0]
    tq = n_ctx
    grp = ATT_QH // ATT_KVH
    scale = ATT_D ** -0.5

    def body(q_ref, k_ref, v_ref, o_ref):
        def run(nk):
            p = _softmax_rows(_dot(q_ref[...], k_ref[0:nk, :], 'nt') * scale)
            o_ref[...] = _dot(p, v_ref[0:nk, :], 'nn')

        pl.when(pl.program_id(2) == 0)(lambda: run(n_ctx))
        pl.when(pl.program_id(2) > 0)(lambda: run(n))

    return pl.pallas_call(
        body, name='attn_fwd', grid=(ATT_KVH, grp, n // tq),
        in_specs=[pl.BlockSpec((tq, ATT_D), lambda h, g, i: (i, h * grp + g)),
                  pl.BlockSpec((n, ATT_D), lambda h, g, i: (0, h)),
                  pl.BlockSpec((n, ATT_D), lambda h, g, i: (0, h))],
        out_specs=pl.BlockSpec((tq, ATT_D), lambda h, g, i: (i, h * grp + g)),
        out_shape=jax.ShapeDtypeStruct((n, ATT_Q), F32),
        compiler_params=_params(("parallel", "parallel", "parallel")),
    )(q, k, v)


def attention_bwd_call(q, k, v, do, n_ctx):
    n = q.shape[0]
    tq = n_ctx
    grp = ATT_QH // ATT_KVH
    scale = ATT_D ** -0.5

    def body(q_ref, k_ref, v_ref, do_ref, dq_ref, dk_ref, dv_ref):
        @pl.when(jnp.logical_and(pl.program_id(1) == 0, pl.program_id(2) == 0))
        def _():
            dk_ref[...] = jnp.zeros_like(dk_ref)
            dv_ref[...] = jnp.zeros_like(dv_ref)

        def run(nk):
            qv, kv, vv, dov = q_ref[...], k_ref[0:nk, :], v_ref[0:nk, :], do_ref[...]
            p = _softmax_rows(_dot(qv, kv, 'nt') * scale)
            dv_ref[0:nk, :] += _dot(p, dov, 'tn')
            dp = _dot(dov, vv, 'nt')
            ds = p * (dp - jnp.sum(dp * p, -1, keepdims=True)) * scale
            dq_ref[...] = _dot(ds, kv, 'nn')
            dk_ref[0:nk, :] += _dot(ds, qv, 'tn')

        pl.when(pl.program_id(2) == 0)(lambda: run(n_ctx))
        pl.when(pl.program_id(2) > 0)(lambda: run(n))

    return pl.pallas_call(
        body, name='attn_bwd', grid=(ATT_KVH, grp, n // tq),
        in_specs=[pl.BlockSpec((tq, ATT_D), lambda h, g, i: (i, h * grp + g)),
                  pl.BlockSpec((n, ATT_D), lambda h, g, i: (0, h)),
                  pl.BlockSpec((n, ATT_D), lambda h, g, i: (0, h)),
                  pl.BlockSpec((tq, ATT_D), lambda h, g, i: (i, h * grp + g))],
        out_specs=[pl.BlockSpec((tq, ATT_D), lambda h, g, i: (i, h * grp + g)),
                   pl.BlockSpec((n, ATT_D), lambda h, g, i: (0, h)),
                   pl.BlockSpec((n, ATT_D), lambda h, g, i: (0, h))],
        out_shape=[jax.ShapeDtypeStruct((n, ATT_Q), F32), jax.ShapeDtypeStruct((n, ATT_KV), F32),
                   jax.ShapeDtypeStruct((n, ATT_KV), F32)],
        compiler_params=_params(("parallel", "arbitrary", "arbitrary")),
    )(q, k, v, do)


def make_attention(n_ctx):
    @jax.custom_vjp
    def attention(q, k, v):
        return attention_fwd_call(q, k, v, n_ctx)

    def fwd(q, k, v):
        return attention_fwd_call(q, k, v, n_ctx), (q, k, v)

    def bwd(res, do):
        return tuple(attention_bwd_call(*res, do, n_ctx))

    attention.defvjp(fwd, bwd)
    return attention


def _log_sigmoid(z):
    return jnp.minimum(z, 0.0) - jnp.log(1.0 + jnp.exp(-jnp.abs(z)))


def _gla_chunk(d, q, k, v, glr, wg, bg, state, mm, tri_dot):
    chunk = q.shape[0]
    r = lax.broadcasted_iota(jnp.int32, (chunk, chunk), 0)
    c = lax.broadcasted_iota(jnp.int32, (chunk, chunk), 1)
    lower = jnp.where(c <= r, 1.0, 0.0).astype(F32)
    upper = jnp.where(c >= r, 1.0, 0.0).astype(F32)
    tri = jnp.where(d == 0, lower, upper)
    tri_t = jnp.where(d == 0, upper, lower)
    log_a = _log_sigmoid(mm(glr, wg, 'nn') + bg) * (1.0 / GLA_TAU)
    outs, new_state = [], []
    for h in range(GLA_HEADS):
        la = log_a[:, h * GLA_DK:(h + 1) * GLA_DK]
        b = tri_dot(tri, tri_t, la)
        tot = jnp.sum(la, axis=0, keepdims=True)
        qh = q[:, h * GLA_DK:(h + 1) * GLA_DK] * (GLA_DK ** -0.5)
        kh = k[:, h * GLA_DK:(h + 1) * GLA_DK]
        vh = v[:, h * GLA_DV:(h + 1) * GLA_DV]
        qe = qh * jnp.exp(b)
        ke = kh * jnp.exp(-b)
        att = mm(qe, ke, 'nt') * tri
        outs.append(mm(att, vh, 'nn') + mm(qe, state[h], 'nn'))
        kd = kh * jnp.exp(tot - b)
        decay = jnp.transpose(jnp.broadcast_to(jnp.exp(tot), (GLA_DK, GLA_DK)))
        decay = jnp.concatenate([decay] * (GLA_DV // GLA_DK), axis=1)
        new_state.append(decay * state[h] + mm(kd, vh, 'tn'))
    return jnp.concatenate(outs, axis=1), tuple(new_state)


def _gla_chunk_of(d, s, nc, ns):
    falling = jnp.where(s < nc, nc - 1 - s, ns - 1 - (s - nc))
    return jnp.where(d == 0, s, falling)


def gla_fwd_call(q, k, v, glr, wg, bg, n_ctx):
    n = q.shape[0]
    ch = GLA_CHUNK
    ns, nc = n // ch, n_ctx // ch

    def at(d, s):
        return _gla_chunk_of(d, s, nc, ns)

    def body(q_ref, k_ref, v_ref, glr_ref, wg_ref, bg_ref, o_ref, saved_ref, state_ref):
        d, s = pl.program_id(0), pl.program_id(1)

        @pl.when(s == 0)
        def _():
            state_ref[...] = jnp.zeros_like(state_ref)

        saved_ref[...] = state_ref[...]
        state = tuple(state_ref[h] for h in range(GLA_HEADS))
        o, new_state = _gla_chunk(d, q_ref[...], k_ref[...], v_ref[...], glr_ref[...], wg_ref[...],
                                  bg_ref[...], state, _dot, _tri_dot_plain)
        o_ref[...] = o
        for h in range(GLA_HEADS):
            state_ref[h] = new_state[h]

    return pl.pallas_call(
        body, name='gla_fwd', grid=(2, ns),
        in_specs=[pl.BlockSpec((ch, GLA_QK), lambda d, s: (at(d, s), 0)),
                  pl.BlockSpec((ch, GLA_QK), lambda d, s: (at(d, s), 0)),
                  pl.BlockSpec((ch, GLA_V), lambda d, s: (at(d, s), 0)),
                  pl.BlockSpec((ch, LANE), lambda d, s: (at(d, s), 0)),
                  pl.BlockSpec((None, LANE, GLA_QK), lambda d, s: (d, 0, 0)),
                  pl.BlockSpec((None, 1, GLA_QK), lambda d, s: (d, 0, 0))],
        out_specs=[pl.BlockSpec((None, ch, GLA_V), lambda d, s: (d, at(d, s), 0)),
                   pl.BlockSpec((None, None, GLA_HEADS, GLA_DK, GLA_DV), lambda d, s: (d, s, 0, 0, 0))],
        out_shape=[jax.ShapeDtypeStruct((2, n, GLA_V), F32),
                   jax.ShapeDtypeStruct((2, ns, GLA_HEADS, GLA_DK, GLA_DV), F32)],
        scratch_shapes=[pltpu.VMEM((GLA_HEADS, GLA_DK, GLA_DV), F32)],
        compiler_params=_params(("parallel", "arbitrary")),
    )(q, k, v, glr, wg, bg)


def gla_bwd_call(q, k, v, glr, wg, bg, saved, do, n_ctx):
    n = q.shape[0]
    ch = GLA_CHUNK
    ns, nc = n // ch, n_ctx // ch

    def at(d, s):
        return _gla_chunk_of(d, ns - 1 - s, nc, ns)

    def body(q_ref, k_ref, v_ref, glr_ref, wg_ref, bg_ref, saved_ref, do_ref,
             dq_ref, dk_ref, dv_ref, dglr_ref, dwg_ref, dbg_ref, dstate_ref):
        d, s = pl.program_id(0), pl.program_id(1)

        @pl.when(s == 0)
        def _():
            dstate_ref[...] = jnp.zeros_like(dstate_ref)
            dwg_ref[...] = jnp.zeros_like(dwg_ref)
            dbg_ref[...] = jnp.zeros_like(dbg_ref)

        state = tuple(saved_ref[h] for h in range(GLA_HEADS))
        dstate = tuple(dstate_ref[h] for h in range(GLA_HEADS))

        def f(qv, kv, vv, glrv, wgv, bgv, st):
            return _gla_chunk(d, qv, kv, vv, glrv, wgv, bgv, st, _mm, _tri_dot)

        _, vjp = jax.vjp(f, q_ref[...], k_ref[...], v_ref[...], glr_ref[...], wg_ref[...], bg_ref[...], state)
        dq, dk, dv, dglr, dwg, dbg, dprev = vjp((do_ref[...], dstate))
        dq_ref[...] = dq
        dk_ref[...] = dk
        dv_ref[...] = dv
        dglr_ref[...] = dglr
        dwg_ref[...] += dwg
        dbg_ref[...] += dbg
        for h in range(GLA_HEADS):
            dstate_ref[h] = dprev[h]

    return pl.pallas_call(
        body, name='gla_bwd', grid=(2, ns),
        in_specs=[pl.BlockSpec((ch, GLA_QK), lambda d, s: (at(d, s), 0)),
                  pl.BlockSpec((ch, GLA_QK), lambda d, s: (at(d, s), 0)),
                  pl.BlockSpec((ch, GLA_V), lambda d, s: (at(d, s), 0)),
                  pl.BlockSpec((ch, LANE), lambda d, s: (at(d, s), 0)),
                  pl.BlockSpec((None, LANE, GLA_QK), lambda d, s: (d, 0, 0)),
                  pl.BlockSpec((None, 1, GLA_QK), lambda d, s: (d, 0, 0)),
                  pl.BlockSpec((None, None, GLA_HEADS, GLA_DK, GLA_DV), lambda d, s: (d, ns - 1 - s, 0, 0, 0)),
                  pl.BlockSpec((None, ch, GLA_V), lambda d, s: (d, at(d, s), 0))],
        out_specs=[pl.BlockSpec((None, ch, GLA_QK), lambda d, s: (d, at(d, s), 0)),
                   pl.BlockSpec((None, ch, GLA_QK), lambda d, s: (d, at(d, s), 0)),
                   pl.BlockSpec((None, ch, GLA_V), lambda d, s: (d, at(d, s), 0)),
                   pl.BlockSpec((None, ch, LANE), lambda d, s: (d, at(d, s), 0)),
                   pl.BlockSpec((None, LANE, GLA_QK), lambda d, s: (d, 0, 0)),
                   pl.BlockSpec((None, 1, GLA_QK), lambda d, s: (d, 0, 0))],
        out_shape=[jax.ShapeDtypeStruct((2, n, GLA_QK), F32), jax.ShapeDtypeStruct((2, n, GLA_QK), F32),
                   jax.ShapeDtypeStruct((2, n, GLA_V), F32), jax.ShapeDtypeStruct((2, n, LANE), F32),
                   jax.ShapeDtypeStruct((2, LANE, GLA_QK), F32), jax.ShapeDtypeStruct((2, 1, GLA_QK), F32)],
        scratch_shapes=[pltpu.VMEM((GLA_HEADS, GLA_DK, GLA_DV), F32)],
        compiler_params=_params(("parallel", "arbitrary")),
    )(q, k, v, glr, wg, bg, saved, do)


def make_gla(n_ctx):
    @jax.custom_vjp
    def gla(q, k, v, glr, wg, bg):
        return gla_fwd_call(q, k, v, glr, wg, bg, n_ctx)[0]

    def fwd(q, k, v, glr, wg, bg):
        o, saved = gla_fwd_call(q, k, v, glr, wg, bg, n_ctx)
        return o, (q, k, v, glr, wg, bg, saved)

    def bwd(res, do):
        dq, dk, dv, dglr, dwg, dbg = gla_bwd_call(*res, do, n_ctx)
        return dq[0] + dq[1], dk[0] + dk[1], dv[0] + dv[1], dglr[0] + dglr[1], dwg, dbg

    gla.defvjp(fwd, bwd)
    return gla


def _s5_chunk_at(kind, g, nch):
    if kind == 0:
        return g
    if kind == 1:
        return jnp.where(g == 0, 0, nch - g)
    if kind == 2:
        return nch - 1 - g
    return jnp.where(g == nch - 1, 0, g + 1)


def s5_scan_call(xr, xi, ar, ai, kind, lc, name):
    nch = xr.shape[0] // lc
    rising = kind in (0, 3)

    def body(xr_ref, xi_ref, ar_ref, ai_ref, sr_ref, si_ref, st_ref):
        g = pl.program_id(0)

        @pl.when(g == 0)
        def _():
            st_ref[...] = jnp.zeros_like(st_ref)

        a_r, a_i = ar_ref[...], ai_ref[...]

        def step(j, carry):
            cr, ci = carry
            tt = j if rising else lc - 1 - j
            nr = a_r * cr - a_i * ci + xr_ref[tt]
            ni = a_r * ci + a_i * cr + xi_ref[tt]
            sr_ref[tt] = nr
            si_ref[tt] = ni
            return nr, ni

        cr, ci = lax.fori_loop(0, lc, step, (st_ref[0], st_ref[1]), unroll=8)
        st_ref[0] = cr
        st_ref[1] = ci

    blk = pl.BlockSpec((lc, S5_ROWS, LANE), lambda g: (_s5_chunk_at(kind, g, nch), 0, 0))
    par = pl.BlockSpec((S5_ROWS, LANE), lambda g: (0, 0))
    return pl.pallas_call(
        body, name=name, grid=(nch,), in_specs=[blk, blk, par, par], out_specs=[blk, blk],
        scratch_shapes=[pltpu.VMEM((2, S5_ROWS, LANE), F32)],
        out_shape=[jax.ShapeDtypeStruct(xr.shape, F32), jax.ShapeDtypeStruct(xr.shape, F32)],
        compiler_params=_params(("arbitrary",)),
    )(xr, xi, ar, ai)


def s5_da_call(lr, li, pr, pi, lc):
    t = lr.shape[0]

    def body(lr_ref, li_ref, pr_ref, pi_ref, dar_ref, dai_ref):
        @pl.when(pl.program_id(0) == 0)
        def _():
            dar_ref[...] = jnp.zeros_like(dar_ref)
            dai_ref[...] = jnp.zeros_like(dai_ref)

        a, b, c, e = lr_ref[...], li_ref[...], pr_ref[...], pi_ref[...]
        dar_ref[...] += jnp.sum(a * c + b * e, axis=0)
        dai_ref[...] += jnp.sum(b * c - a * e, axis=0)

    blk = pl.BlockSpec((lc, S5_ROWS, LANE), lambda g: (g, 0, 0))
    par = pl.BlockSpec((S5_ROWS, LANE), lambda g: (0, 0))
    return pl.pallas_call(
        body, name='s5_da', grid=(t // lc,), in_specs=[blk] * 4, out_specs=[par, par],
        out_shape=[jax.ShapeDtypeStruct((S5_ROWS, LANE), F32)] * 2,
        compiler_params=_params(("arbitrary",)),
    )(lr, li, pr, pi)


def make_s5_scan(direction, n_chunks, lc):
    def predecessors(s):
        zero = jnp.zeros((1,) + s.shape[1:], F32)
        if direction == 0:
            return jnp.concatenate([zero, s[:-1]])
        return jnp.concatenate([s[1:lc], zero, s[lc + 1:], s[0:1]])

    @jax.custom_vjp
    def scan(xr, xi, ar, ai):
        return tuple(s5_scan_call(xr, xi, ar, ai, direction, lc, 's5_scan_d%d' % direction))

    def fwd(xr, xi, ar, ai):
        sr, si = s5_scan_call(xr, xi, ar, ai, direction, lc, 's5_scan_d%d' % direction)
        return (sr, si), (sr, si, ar, ai)

    def bwd(res, g):
        sr, si, ar, ai = res
        lr, li = s5_scan_call(g[0], g[1], ar, -ai, 2 + direction, lc, 's5_adjoint_d%d' % direction)
        dar, dai = s5_da_call(lr, li, predecessors(sr), predecessors(si), lc)
        return lr, li, dar, dai

    scan.defvjp(fwd, bwd)
    return scan


def loss_fwd_call(y, target, tile):
    n, dm = y.shape

    def body(y_ref, t_ref, o_ref):
        e = y_ref[...] - t_ref[...]
        o_ref[...] = jnp.full(o_ref.shape, 0.5 * jnp.sum(e * e) / dm, F32)

    out = pl.pallas_call(
        body, name='loss_fwd', grid=(n // tile,),
        in_specs=[pl.BlockSpec((tile, dm), lambda i: (i, 0))] * 2,
        out_specs=pl.BlockSpec((8, LANE), lambda i: (i, 0)),
        out_shape=jax.ShapeDtypeStruct((8 * (n // tile), LANE), F32),
        compiler_params=_params(("parallel",)),
    )(y, target)
    return jnp.sum(out[::8, 0])


def make_loss(tile):
    @jax.custom_vjp
    def loss(y, target):
        return loss_fwd_call(y, target, tile)

    def fwd(y, target):
        return loss_fwd_call(y, target, tile), (y, target)

    def bwd(res, g):
        y, target = res
        scale = jnp.full((1, y.shape[1]), g / y.shape[1], F32)

        def body(y_ref, t_ref, s_ref, o_ref):
            o_ref[...] = (y_ref[...] - t_ref[...]) * s_ref[...]

        dy = pl.pallas_call(
            body, name='loss_bwd', grid=(y.shape[0] // tile,),
            in_specs=[pl.BlockSpec((tile, y.shape[1]), lambda i: (i, 0))] * 2 +
                     [pl.BlockSpec((1, y.shape[1]), lambda i: (0, 0))],
            out_specs=pl.BlockSpec((tile, y.shape[1]), lambda i: (i, 0)),
            out_shape=jax.ShapeDtypeStruct(y.shape, F32),
            compiler_params=_params(("parallel",)),
        )(y, target, scale)
        return dy, jnp.zeros_like(target)

    loss.defvjp(fwd, bwd)
    return loss


def all_gather(x, name):
    return exchange_call('gather', x, name)


def all_to_all(x, name):
    return exchange_call('scatter', x, name)


def exchange_call(kind, x, name):
    def body(x_ref, out_ref, send_sems, recv_sems, local_sem):
        refs = (kind, x_ref, out_ref, send_sems, recv_sems, local_sem)
        _exchange_start(*refs)
        _exchange_wait(*refs)

    return pl.pallas_call(
        body, name=name, out_shape=jax.ShapeDtypeStruct(_exchange_out_shape(kind, x), x.dtype),
        in_specs=[pl.BlockSpec(memory_space=pl.ANY)], out_specs=pl.BlockSpec(memory_space=pl.ANY),
        scratch_shapes=_EXCHANGE_SEMS,
    )(x)


def add_halves_call(a, b, name):
    p, r, c = a.shape
    tr = _pick(r, 256, 16)

    def body(a_ref, b_ref, o_ref):
        o_ref[...] = (a_ref[...].astype(F32) + b_ref[...].astype(F32)).astype(o_ref.dtype)

    blk = pl.BlockSpec((None, tr, c), lambda q, i: (q, i, 0))
    return pl.pallas_call(
        body, name=name, grid=(p, r // tr), in_specs=[blk, blk], out_specs=blk,
        out_shape=jax.ShapeDtypeStruct(a.shape, BF16), compiler_params=_params(("parallel", "parallel")),
    )(a, b)


def reduce_scatter_parts(g, name):
    c = lax.axis_index("c")
    mine = lax.dynamic_index_in_dim(g.reshape((N_DEV // 2, 2) + g.shape[1:]), c, axis=1, keepdims=False)
    return exchange_call('quad', add_halves_call(mine, exchange_call('pair', g, name + '_pair'), name + '_add'),
                         name + '_quad')


def sum_parts_call(parts, name):
    p, r, c = parts.shape
    tr = _pick(r, 256, 8)

    def body(p_ref, o_ref):
        acc = p_ref[0].astype(F32)
        for j in range(1, p):
            acc = acc + p_ref[j].astype(F32)
        o_ref[...] = acc

    return pl.pallas_call(
        body, name=name, grid=(r // tr,), in_specs=[pl.BlockSpec((p, tr, c), lambda i: (0, i, 0))],
        out_specs=pl.BlockSpec((tr, c), lambda i: (i, 0)), out_shape=jax.ShapeDtypeStruct((r, c), F32),
        compiler_params=_params(("parallel",)),
    )(parts)


@jax.custom_vjp
def gather_rows(x):
    return all_gather(x, 'mod_gather')


def _gather_rows_fwd(x):
    return all_gather(x, 'mod_gather'), None


def _gather_rows_bwd(_, ct):
    return (sum_parts_call(all_to_all(ct, 'mod_scatter'), 'mod_scatter_sum'),)


gather_rows.defvjp(_gather_rows_fwd, _gather_rows_bwd)


def adamw_call(parts, w, m, v, name):
    p, r, c = parts.shape
    tr = _pick(r, 128, 8)
    c1 = 1.0 / (1.0 - ADAM_B1 ** ADAM_STEP)
    c2 = 1.0 / (1.0 - ADAM_B2 ** ADAM_STEP)

    def body(p_ref, w_ref, m_ref, v_ref, g_ref, d_ref, m2_ref, v2_ref):
        g = p_ref[0].astype(F32)
        for j in range(1, p):
            g = g + p_ref[j].astype(F32)
        m2 = ADAM_B1 * m_ref[...] + (1.0 - ADAM_B1) * g
        v2 = ADAM_B2 * v_ref[...] + (1.0 - ADAM_B2) * (g * g)
        g_ref[...] = g
        m2_ref[...] = m2
        v2_ref[...] = v2
        d_ref[...] = -ADAM_LR * ((m2 * c1) / (jnp.sqrt(v2 * c2) + ADAM_EPS) + ADAM_WD * w_ref[...])

    blk = pl.BlockSpec((tr, c), lambda i: (i, 0))
    return pl.pallas_call(
        body, name=name, grid=(r // tr,),
        in_specs=[pl.BlockSpec((p, tr, c), lambda i: (0, i, 0)), blk, blk, blk], out_specs=[blk] * 4,
        out_shape=[jax.ShapeDtypeStruct((r, c), F32)] * 4,
        compiler_params=_params(("parallel",)),
    )(parts, w, m, v)


def _deinterleave(z, heads):
    lead = z.shape[:-1]
    return z.reshape(lead + (heads, ATT_D // 2, 2)).swapaxes(-1, -2).reshape(lead + (heads * ATT_D,))


def _unshard_cols(g):
    return jnp.moveaxis(g, 0, -2).reshape(g.shape[1:-1] + (N_DEV * g.shape[-1],))


def _pack_w_in(g, dm):
    w = _unshard_cols(g)
    splits = (GLA_QK, GLA_QK, GLA_V, GLA_V, GLA_RANK, S5_WIDTH, ATT_Q, ATT_KV, ATT_KV, 3 * dm)
    gq, gk, gv, gr, glr, su, aq, ak, av, bg = jnp.split(w, np.cumsum(splits)[:-1].tolist(), axis=-1)
    pad = jnp.zeros(w.shape[:-1] + (LANE - GLA_RANK,), w.dtype)
    return jnp.concatenate([gq, gk, gv, gr, su, _deinterleave(aq, ATT_QH), _deinterleave(ak, ATT_KVH),
                            av, bg, glr, pad], axis=-1)


def _in_widths(dm):
    return (GLA_QK, GLA_QK, GLA_V, GLA_V, S5_WIDTH, ATT_Q, ATT_KV, ATT_KV, 3 * dm, LANE)


def _make_split(widths):
    cuts = np.cumsum(widths)[:-1].tolist()

    @jax.custom_vjp
    def split(z):
        return tuple(jnp.split(z, cuts, axis=1))

    def fwd(z):
        return tuple(jnp.split(z, cuts, axis=1)), None

    def bwd(_, cots):
        return (jnp.concatenate(cots, axis=1),)

    split.defvjp(fwd, bwd)
    return split


def _block_diag_in(b):
    eye = jnp.eye(S5_GROUPS, dtype=F32)
    return jnp.einsum('gpc,gh->gchp', b, eye).reshape(S5_WIDTH, S5_CH)


def _block_diag_out(cm):
    eye = jnp.eye(S5_GROUPS, dtype=F32)
    return jnp.einsum('gcp,gh->gphc', cm, eye).reshape(S5_CH, S5_WIDTH)


def _s5_discretise(lam_re, lam_im, log_dt):
    dt = jnp.exp(log_dt)[:, None]
    mag = jnp.exp(lam_re * dt)
    a_re, a_im = mag * jnp.cos(lam_im * dt), mag * jnp.sin(lam_im * dt)
    den = lam_re * lam_re + lam_im * lam_im
    nr, ni = a_re - 1, a_im
    k_re = (nr * lam_re + ni * lam_im) / den
    k_im = (ni * lam_re - nr * lam_im) / den
    return a_re, a_im, k_re, k_im


def _rope_tables(n_ctx, n_lat):
    rows = jnp.repeat(jnp.arange(n_lat // GRID_W), GRID_W).astype(F32)
    cols = jnp.tile(jnp.arange(GRID_W), n_lat // GRID_W).astype(F32)
    n_freq = ATT_D // 4
    inv = ROPE_THETA ** (-jnp.arange(n_freq, dtype=F32) / n_freq)
    ang = jnp.concatenate([rows[:, None] * inv, cols[:, None] * inv], -1)
    cos, sin = jnp.cos(ang), jnp.sin(ang)
    cosf = jnp.concatenate([jnp.ones((n_ctx, ATT_D), F32), jnp.concatenate([cos, cos], -1)], 0)
    sinf = jnp.concatenate([jnp.zeros((n_ctx, ATT_D), F32), jnp.concatenate([-sin, sin], -1)], 0)
    return cosf, sinf


def _build_ops(n, n_ctx, dm, dff, depth):
    t256 = min(256, n_ctx)
    t64 = 64
    alpha = (2 * depth) ** 0.25
    ops = {}

    def seg(i, tile, pc, pl_):
        return jnp.where(i < n_ctx // tile, pc, pl_)

    def modulate_f(i, rows, params):
        sh = seg(i, t256, params[0], params[1])
        sc = seg(i, t256, params[2], params[3])
        return (rows[0] * (1.0 + sc) + sh,)

    def postnorm_f(i, rows, params):
        g = seg(i, t256, params[0], params[1])
        return (_layer_norm(alpha * rows[0] + g * rows[1], params[2], params[3]),)

    def gla_out_f(i, rows, params):
        o = rows[0] + rows[1]
        heads = [_layer_norm(o[:, h * GLA_DV:(h + 1) * GLA_DV], None, None) for h in range(GLA_HEADS)]
        return (jnp.concatenate(heads, axis=1) * params[0] * _silu(rows[2]),)

    def s5_in_f(i, rows, params):
        br, bi = rows[0][:, :S5_CH], rows[0][:, S5_CH:]
        kr0, ki0, kr1, ki1 = params
        return (kr0 * br - ki0 * bi, kr0 * bi + ki0 * br, kr1 * br - ki1 * bi, kr1 * bi + ki1 * br)

    def s5_mid_f(i, rows, params):
        return (jax.nn.gelu(rows[0] + rows[1] * params[0]),)

    def s5_gate_f(i, rows, params):
        return (rows[0] * jax.nn.sigmoid(rows[1]),)

    def qk_f(swap):
        def f(i, rows, params):
            aq, ak, cosf, sinf = rows
            outs = []
            for z, w, heads in ((aq, params[0], ATT_QH), (ak, params[1], ATT_KVH)):
                hs = []
                for h in range(heads):
                    zh = z[:, h * ATT_D:(h + 1) * ATT_D]
                    zn = zh * lax.rsqrt(jnp.mean(zh * zh, -1, keepdims=True) + EPS) * w
                    hs.append(zn * cosf + swap(zn) * sinf)
                outs.append(jnp.concatenate(hs, axis=1))
            return tuple(outs)
        return f

    def silu_f(i, rows, params):
        return (_silu(rows[0]),)

    def merge_f(i, rows, params):
        pg, ps, pa, bg = rows
        gate = jax.nn.sigmoid(bg)
        return (gate[:, :dm] * pg + gate[:, dm:2 * dm] * ps + gate[:, 2 * dm:] * pa,)

    def swiglu_f(i, rows, params):
        return (_silu(rows[0]) * rows[1],)

    ops['modulate1'] = make_rowwise('modulate1', modulate_f, (dm,), t256)
    ops['modulate2'] = make_rowwise('modulate2', modulate_f, (dm,), t256)
    ops['postnorm1'] = make_rowwise('postnorm1', postnorm_f, (dm,), t256)
    ops['postnorm2'] = make_rowwise('postnorm2', postnorm_f, (dm,), t256)
    ops['gla_out'] = make_rowwise('gla_out', gla_out_f, (GLA_V,), t256)
    ops['s5_in'] = make_rowwise('s5_in', s5_in_f, (S5_CH,) * 4, t64)
    ops['s5_mid'] = make_rowwise('s5_mid', s5_mid_f, (S5_WIDTH,), t256)
    ops['s5_gate'] = make_rowwise('s5_gate', s5_gate_f, (S5_WIDTH,), t256)
    ops['merge'] = make_rowwise('merge', merge_f, (dm,), t64)
    ops['swiglu'] = make_rowwise('swiglu', swiglu_f, (dff,), t64)
    ops['qk_prep'] = make_rowwise('qk_prep', qk_f(_swap_plain), (ATT_Q, ATT_KV), t256, n_nondiff=2,
                                  f_diff=qk_f(_swap_halves))
    ops['silu_c'] = make_rowwise('silu_c', silu_f, (dm,), 16)
    return ops


def _flat2d(a):
    return a.reshape(-1, a.shape[-1])


def _step(given):
    x, c, ctx = given['x'][0], given['c'], given['ctx'][0]
    target = given['loss_target'][0]
    n_lat, dm = x.shape
    n_ctx = ctx.shape[0]
    n = n_ctx + n_lat
    depth = given['w_ada'].shape[0]
    dff = given['w_ffn_out'].shape[1] * N_DEV
    ada_cols = given['w_ada'].shape[2]
    xi, yi, ci = _coords()
    me = 4 * xi + 2 * yi + ci
    lc = n_ctx
    ops = _build_ops(n, n_ctx, dm, dff, depth)
    names = ['ada', 'in', 's5b', 's5c', 'glu', 'pg', 'ps', 'pa', 'out', 'fa', 'fb', 'fo']
    lin = {k: make_linear('lin_' + k) for k in names}
    split_in = _make_split(_in_widths(dm))
    gla = make_gla(n_ctx)
    attention = make_attention(n_ctx)
    scans = [make_s5_scan(d, n // lc, lc) for d in range(2)]
    loss_op = make_loss(min(256, n_lat))
    cosf, sinf = _rope_tables(n_ctx, n_lat)

    carrier = {k: make_carrier_linear('lin_' + k) for k in ('in', 'fa', 'fo', 'out')}
    carried_by = {'w_in': 'in', 'w_ffn_in': 'fa', 'w_ffn_out': 'fo', 'misc': 'out'}

    def misc_block(src, prefix, l, dtype):
        return jnp.concatenate([src[prefix + k][l].astype(dtype).reshape(-1) for k in MISC]).reshape(-1, LANE)

    blocks = [{'w_in': given['w_in'][l].astype(BF16), 'w_ffn_in': given['w_ffn_in'][l].astype(BF16),
               'w_ffn_out': given['w_ffn_out'][l].astype(BF16), 'misc': misc_block(given, '', l, BF16)}
              for l in range(depth)]
    gathered0 = {k: all_gather(blocks[0][k], 'gather_' + k) for k in FAMILIES}
    gates = {k: all_gather(given[k], 'gather_' + k) for k in ('w_gla_gate', 'b_gla_gate')}
    c_all = all_gather(c, 'gather_c').reshape(N_DEV, dm)

    diff0 = {
        'x': x,
        'rep': {k: given[k] for k in REPLICATED},
        'grads': [{k: jnp.zeros((N_DEV if l == 0 else N_DEV // 2,) + blocks[l][k].shape, BF16) for k in FAMILIES}
                  for l in range(depth)],
        'gates': {k: jnp.zeros((N_DEV,) + given[k].shape, F32) for k in gates},
        'ada': jnp.zeros(given['w_ada'].shape, F32),
    }
    w_ada_bf = given['w_ada'].astype(BF16)
    misc_shapes = [given[k].shape[1:] for k in MISC]
    misc_sizes = [int(np.prod(s)) for s in misc_shapes]

    def unpack(g):
        parts = jnp.split(g['misc'].reshape(N_DEV, -1), np.cumsum(misc_sizes)[:-1].tolist(), axis=1)
        m = {k: p.reshape((N_DEV,) + s) for k, p, s in zip(MISC, parts, misc_shapes)}
        w_fi = _unshard_cols(g['w_ffn_in'])
        rows = lambda z: z.reshape((N_DEV * z.shape[1],) + z.shape[2:])
        return dict(w_in=_pack_w_in(g['w_in'], dm), w_fa=w_fi[:, :dff], w_fb=w_fi[:, dff:],
                    w_fo=rows(g['w_ffn_out']), w_out=rows(m['w_out']), w_pg=_unshard_cols(m['w_proj_gla']),
                    w_ps=_unshard_cols(m['w_proj_s5']), w_pa=_unshard_cols(m['w_proj_attn']),
                    w_glu=rows(m['w_s5_glu']))

    def loss_fn(diff):
        r = diff['rep']
        wgate = _unshard_cols(gates['w_gla_gate'] + diff['gates']['w_gla_gate'])
        wgate = jnp.pad(wgate, ((0, 0), (0, 0), (0, LANE - GLA_RANK), (0, 0)))
        bgate = _unshard_cols(gates['b_gla_gate'] + diff['gates']['b_gla_gate'])[:, :, None, :]

        disc = jax.vmap(jax.vmap(_s5_discretise))(r['s5_lam_re'], r['s5_lam_im'], r['s5_log_dt'])
        a_re, a_im = (z.reshape(depth, 2, S5_ROWS, LANE) for z in disc[:2])
        k_re, k_im = (z.reshape(depth, 2, 1, S5_CH) for z in disc[2:])
        b_full = jnp.concatenate([jax.vmap(_block_diag_in)(r['s5_b_re']), jax.vmap(_block_diag_in)(r['s5_b_im'])], -1)
        bd_out = jax.vmap(_block_diag_out)
        c_full = jnp.concatenate([bd_out(r['s5_c_re'][:, 0]), -bd_out(r['s5_c_im'][:, 0]),
                                  bd_out(r['s5_c_re'][:, 1]), -bd_out(r['s5_c_im'][:, 1])], 1)
        row = lambda z: z[:, None, :]
        qw, kw = row(_deinterleave(r['q_norm_w'], 1)), row(_deinterleave(r['k_norm_w'], 1))

        c16 = jnp.concatenate([c_all, r['c_ctx'][None], jnp.zeros((7, dm), F32)], 0)
        c16s, = ops['silu_c']((c16,), ())

        small = dict(w_ada=w_ada_bf, p_ada=diff['ada'], b_ada=r['b_ada'], wgate=wgate, bgate=bgate,
                     a_re=a_re, a_im=a_im, k_re=k_re, k_im=k_im, b_full=b_full, c_full=c_full, qw=qw, kw=kw,
                     gnw=row(r['gla_norm_w']), s5_d=row(r['s5_d']), ln1_w=row(r['ln1_w']), ln1_b=row(r['ln1_b']),
                     ln2_w=row(r['ln2_w']), ln2_b=row(r['ln2_b']))
        family_of = {v: f for f, v in carried_by.items()}

        def matmul(k, a, p, nxt, got):
            if nxt is None or k not in family_of:
                return lin[k](a, p['w_' + k], p['p_' + k])
            fam = family_of[k]
            out, got['w'][fam], got['g'][fam] = carrier[k](a, p['w_' + k], p['p_' + k], nxt[0][fam], nxt[1][fam])
            return out

        def layer(xc, p, nxt, got):
            msh = lin['ada'](c16s, p['w_ada'], p['p_ada']) + lax.dynamic_slice(p['b_ada'], (me * ada_cols,), (ada_cols,))
            mods = gather_rows(msh)
            m_lat = lax.dynamic_index_in_dim(mods, me, axis=1, keepdims=False).reshape(1, 6 * dm)
            m_ctx = mods[:, N_DEV, :].reshape(1, 6 * dm)
            sh1, sc1, g1, sh2, sc2, g2 = [(m_ctx[:, j * dm:(j + 1) * dm], m_lat[:, j * dm:(j + 1) * dm]) for j in range(6)]

            h, = ops['modulate1']((xc,), sh1 + sc1)
            gq, gk, gv, gr, su, aq, ak, av, bgt, glr = split_in(matmul('in', h, p, nxt, got))
            o2 = gla(gq, gk, gv, glr, p['wgate'], p['bgate'])
            o_gla, = ops['gla_out']((o2[0], o2[1], gr), (p['gnw'],))

            bu = lin['s5b'](su, p['b_full'].astype(BF16), p['b_full'])
            xin = ops['s5_in']((bu,), (p['k_re'][0], p['k_im'][0], p['k_re'][1], p['k_im'][1]))
            tile3 = lambda z: z.reshape(n, S5_ROWS, LANE)
            st = []
            for d in range(2):
                st += scans[d](tile3(xin[2 * d]), tile3(xin[2 * d + 1]), p['a_re'][d], p['a_im'][d])
            states = jnp.concatenate([z.reshape(n, S5_CH) for z in st], axis=1)
            y0 = lin['s5c'](states, p['c_full'].astype(BF16), p['c_full'])
            ya, = ops['s5_mid']((y0, su), (p['s5_d'],))
            o_s5, = ops['s5_gate']((ya, lin['glu'](ya, p['w_glu'], p['p_glu'])), ())

            q, k = ops['qk_prep']((aq, ak, cosf, sinf), (p['qw'], p['kw']))
            o_att = attention(q, k, av)

            merged, = ops['merge']((lin['pg'](o_gla, p['w_pg'], p['p_pg']), lin['ps'](o_s5, p['w_ps'], p['p_ps']),
                                    lin['pa'](o_att, p['w_pa'], p['p_pa']), bgt), ())
            mix = matmul('out', merged, p, nxt, got)
            x1, = ops['postnorm1']((xc, mix), g1 + (p['ln1_w'], p['ln1_b']))
            h2, = ops['modulate2']((x1,), sh2 + sc2)
            act, = ops['swiglu']((matmul('fa', h2, p, nxt, got), lin['fb'](h2, p['w_fb'], p['p_fb'])), ())
            x2, = ops['postnorm2']((x1, matmul('fo', act, p, nxt, got)), g2 + (p['ln2_w'], p['ln2_b']))
            return x2

        xc = jnp.concatenate([ctx, diff['x']], 0)
        weights, stand_ins = gathered0, diff['grads'][0]
        for l in range(depth):
            p = {k: v[l] for k, v in small.items()}
            p.update(unpack(weights))
            p.update({'p' + k[1:]: v for k, v in unpack({k: v.astype(F32) for k, v in stand_ins.items()}).items()})
            nxt = (blocks[l + 1], diff['grads'][l + 1]) if l + 1 < depth else None
            got = {'w': {}, 'g': {}}
            xc = layer(xc, p, nxt, got)
            weights, stand_ins = {k: lax.stop_gradient(v) for k, v in got['w'].items()}, got['g']
        return loss_op(xc[n_ctx:], target)

    loss_local, grads = jax.value_and_grad(loss_fn)(diff0)
    loss = lax.psum(loss_local, ("x", "y", "c"))

    out = {}

    def update(k, parts):
        w2 = _flat2d(given[k])
        res = adamw_call(parts.reshape((parts.shape[0],) + w2.shape), w2, _flat2d(given['m_' + k]),
                         _flat2d(given['v_' + k]), 'adamw_' + k)
        out[k] = [z.reshape(given[k].shape) for z in res]

    parts = {k: jnp.stack([reduce_scatter_parts(grads['grads'][0][k], 'scatter_' + k)] +
                          [grads['grads'][l][k] for l in range(1, depth)], axis=1) for k in FAMILIES}
    for k in ('w_in', 'w_ffn_in', 'w_ffn_out'):
        update(k, parts[k])
    stacked = lambda prefix: jnp.concatenate([misc_block(given, prefix, l, F32) for l in range(depth)])
    res = adamw_call(parts['misc'].reshape(N_DEV // 2, -1, LANE), stacked(''), stacked('m_'), stacked('v_'),
                     'adamw_misc')
    offs = np.cumsum([0] + misc_sizes)
    for j, k in enumerate(MISC):
        out[k] = [z.reshape(depth, -1)[:, offs[j]:offs[j + 1]].reshape(given[k].shape) for z in res]
    for k in gates:
        update(k, all_to_all(grads['gates'][k], 'scatter_' + k))
    update('w_ada', grads['ada'][None])

    sizes = [int(np.prod(given[k].shape)) for k in REPLICATED]
    total = sum(sizes)
    wide = 8 * LANE
    padded = -(-total // (8 * wide)) * (8 * wide)

    def flat(prefix, src):
        v = jnp.concatenate([src[prefix + k].reshape(-1) for k in REPLICATED] + [jnp.zeros((padded - total,), F32)])
        return v.reshape(padded // wide, wide)

    g_all = all_gather(flat('', grads['rep']), 'gather_small_grads')
    res = adamw_call(g_all, flat('', given), flat('m_', given), flat('v_', given), 'adamw_small')
    offs = np.cumsum([0] + sizes)
    for j, k in enumerate(REPLICATED):
        out[k] = [z.reshape(-1)[offs[j]:offs[j + 1]].reshape(given[k].shape) for z in res]

    grad_x = grads['x'][None]
    return (loss, grad_x) + tuple(out[k][j] for j in range(4) for k in WEIGHTS)


def kernel(x, c, ctx, c_ctx, w_ada, b_ada, w_in, w_gla_gate, b_gla_gate, gla_norm_w, s5_lam_re, s5_lam_im, s5_log_dt, s5_b_re, s5_b_im, s5_c_re, s5_c_im, s5_d, w_s5_glu, q_norm_w, k_norm_w, w_proj_gla, w_proj_s5, w_proj_attn, w_out, ln1_w, ln1_b, ln2_w, ln2_b, w_ffn_in, w_ffn_out, loss_target, m_c_ctx, m_w_ada, m_b_ada, m_w_in, m_w_gla_gate, m_b_gla_gate, m_gla_norm_w, m_s5_lam_re, m_s5_lam_im, m_s5_log_dt, m_s5_b_re, m_s5_b_im, m_s5_c_re, m_s5_c_im, m_s5_d, m_w_s5_glu, m_q_norm_w, m_k_norm_w, m_w_proj_gla, m_w_proj_s5, m_w_proj_attn, m_w_out, m_ln1_w, m_ln1_b, m_ln2_w, m_ln2_b, m_w_ffn_in, m_w_ffn_out, v_c_ctx, v_w_ada, v_b_ada, v_w_in, v_w_gla_gate, v_b_gla_gate, v_gla_norm_w, v_s5_lam_re, v_s5_lam_im, v_s5_log_dt, v_s5_b_re, v_s5_b_im, v_s5_c_re, v_s5_c_im, v_s5_d, v_w_s5_glu, v_q_norm_w, v_k_norm_w, v_w_proj_gla, v_w_proj_s5, v_w_proj_attn, v_w_out, v_ln1_w, v_ln1_b, v_ln2_w, v_ln2_b, v_w_ffn_in, v_w_ffn_out):
    return _step(dict(locals()))
```

```python
import functools
import math

import numpy as np
import jax
import jax.numpy as jnp
from jax import lax
from jax.experimental import pallas as pl
from jax.experimental.pallas import tpu as pltpu

F32 = jnp.float32
BF16 = jnp.bfloat16
MESH = pl.DeviceIdType.MESH
N_DEV = 8
VMEM_LIMIT_V7X = 48 * 1024 * 1024
LANE = 128
SINGLE_STEP_K = 2304

GRID_W = 64
GLA_HEADS, GLA_DK, GLA_DV = 4, 128, 256
GLA_QK, GLA_V = GLA_HEADS * GLA_DK, GLA_HEADS * GLA_DV
GLA_RANK, GLA_TAU, GLA_CHUNK = 16, 16.0, 64
S5_WIDTH, S5_GROUP, S5_GROUPS, S5_STATE = 768, 16, 48, 64
S5_CH = S5_GROUPS * S5_STATE
S5_ROWS = S5_CH // LANE
ATT_QH, ATT_KVH, ATT_D = 8, 2, 128
ATT_Q, ATT_KV = ATT_QH * ATT_D, ATT_KVH * ATT_D
ROPE_THETA = 10000.0
EPS = 1e-6
ADAM_LR, ADAM_B1, ADAM_B2, ADAM_EPS, ADAM_WD, ADAM_STEP = 0.001, 0.9, 0.999, 1e-08, 0.01, 10

WEIGHTS = ['c_ctx', 'w_ada', 'b_ada', 'w_in', 'w_gla_gate', 'b_gla_gate', 'gla_norm_w', 's5_lam_re',
           's5_lam_im', 's5_log_dt', 's5_b_re', 's5_b_im', 's5_c_re', 's5_c_im', 's5_d', 'w_s5_glu',
           'q_norm_w', 'k_norm_w', 'w_proj_gla', 'w_proj_s5', 'w_proj_attn', 'w_out', 'ln1_w', 'ln1_b',
           'ln2_w', 'ln2_b', 'w_ffn_in', 'w_ffn_out']
GATHERED = ['w_in', 'w_gla_gate', 'b_gla_gate', 'w_s5_glu', 'w_proj_gla', 'w_proj_s5', 'w_proj_attn',
            'w_out', 'w_ffn_in', 'w_ffn_out']
REPLICATED = [n for n in WEIGHTS if n not in GATHERED and n != 'w_ada']
FAMILIES = ['w_in', 'w_ffn_in', 'w_ffn_out', 'w_out', 'w_proj_gla', 'w_proj_s5', 'w_proj_attn', 'w_s5_glu']


def _params(sem=None):
    return pltpu.CompilerParams(dimension_semantics=sem, vmem_limit_bytes=VMEM_LIMIT_V7X)


def _pick(dim, target, align):
    best = None
    t = align
    while t <= min(dim, target):
        if dim % t == 0:
            best = t
        t += align
    return dim if best is None else best


_CONTRACT = {'nn': ((1,), (0,)), 'nt': ((1,), (1,)), 'tn': ((0,), (0,))}


def _dot(a, b, dims):
    return lax.dot_general(a.astype(BF16), b.astype(BF16), (_CONTRACT[dims], ((), ())),
                           preferred_element_type=F32)


@functools.partial(jax.custom_vjp, nondiff_argnums=(2,))
def _mm(a, b, dims):
    return _dot(a, b, dims)


def _mm_fwd(a, b, dims):
    return _dot(a, b, dims), (a, b)


def _mm_bwd(dims, res, g):
    a, b = res
    if dims == 'nn':
        return _dot(g, b, 'nt'), _dot(a, g, 'tn')
    if dims == 'nt':
        return _dot(g, b, 'nn'), _dot(g, a, 'tn')
    return _dot(b, g, 'nt'), _dot(a, g, 'nn')


_mm.defvjp(_mm_fwd, _mm_bwd)


@jax.custom_vjp
def _tri_dot(tri, tri_t, z):
    return lax.dot_general(tri, z, (((1,), (0,)), ((), ())), precision=lax.Precision.HIGHEST,
                           preferred_element_type=F32)


def _tri_dot_fwd(tri, tri_t, z):
    return _tri_dot(tri, tri_t, z), (tri, tri_t)


def _tri_dot_bwd(res, g):
    tri, tri_t = res
    dz = lax.dot_general(tri_t, g, (((1,), (0,)), ((), ())), precision=lax.Precision.HIGHEST,
                         preferred_element_type=F32)
    return jnp.zeros_like(tri), jnp.zeros_like(tri_t), dz


_tri_dot.defvjp(_tri_dot_fwd, _tri_dot_bwd)


def _tri_dot_plain(tri, tri_t, z):
    return lax.dot_general(tri, z, (((1,), (0,)), ((), ())), precision=lax.Precision.HIGHEST,
                           preferred_element_type=F32)


@jax.custom_vjp
def _swap_halves(x):
    return pltpu.roll(x, LANE // 2, 1)


def _swap_fwd(x):
    return pltpu.roll(x, LANE // 2, 1), None


def _swap_bwd(_, g):
    return (pltpu.roll(g, LANE // 2, 1),)


_swap_halves.defvjp(_swap_fwd, _swap_bwd)


def _swap_plain(x):
    return pltpu.roll(x, LANE // 2, 1)


def _coords():
    return lax.axis_index("x"), lax.axis_index("y"), lax.axis_index("c")


def _exchange_out_shape(kind, x):
    if kind == 'gather':
        return (N_DEV,) + x.shape
    return (N_DEV // 2,) + x.shape[1:] if kind == 'pair' else x.shape


def _exchange_copies(kind, x_ref, out_ref, send_sems, recv_sems, local_sem):
    x, y, c = _coords()
    chip, me = 2 * x + y, 4 * x + 2 * y + c
    sibling = (x, y, 1 - c)
    others = [(1 - x, y), (x, 1 - y), (1 - x, 1 - y)]

    def remote(k, src, dst, to):
        return pltpu.make_async_remote_copy(src_ref=src, dst_ref=dst, send_sem=send_sems.at[k],
                                            recv_sem=recv_sems.at[k], device_id=to, device_id_type=MESH)

    if kind == 'gather':
        slot = lambda px, py, pc: out_ref.at[4 * px + 2 * py + pc]
        local = pltpu.make_async_copy(x_ref, out_ref.at[me], local_sem)
        first = [remote(0, x_ref, out_ref.at[me], sibling)]
        first += [remote(1 + j, x_ref, out_ref.at[me], (px, py, c)) for j, (px, py) in enumerate(others)]
        landed = [(remote(1 + j, x_ref, slot(px, py, c), sibling),
                   remote(4 + j, slot(px, py, c), slot(px, py, c), sibling)) for j, (px, py) in enumerate(others)]
        last = [remote(0, x_ref, slot(x, y, 1 - c), sibling)]
        last += [remote(4 + j, x_ref, slot(px, py, 1 - c), sibling) for j, (px, py) in enumerate(others)]
        return local, first, landed, last
    if kind == 'scatter':
        local = pltpu.make_async_copy(x_ref.at[me], out_ref.at[me], local_sem)
        first = []
        for k in range(1, N_DEV):
            px, py, pc = (1 - x if k & 4 else x), (1 - y if k & 2 else y), (1 - c if k & 1 else c)
            first.append(remote(k - 1, x_ref.at[4 * px + 2 * py + pc], out_ref.at[me], (px, py, pc)))
        return local, first, [], first
    if kind == 'pair':
        first = [remote(q, x_ref.at[2 * q + 1 - c], out_ref.at[q], sibling) for q in range(N_DEV // 2)]
        return None, first, [], first
    assert kind == 'quad', kind
    local = pltpu.make_async_copy(x_ref.at[chip], out_ref.at[chip], local_sem)
    first = [remote(j, x_ref.at[2 * px + py], out_ref.at[chip], (px, py, c)) for j, (px, py) in enumerate(others)]
    return local, first, [], first


def _exchange_start(*refs):
    local, first, _, _ = _exchange_copies(*refs)
    if local is not None:
        local.start()
    for cp in first:
        cp.start()


def _exchange_wait(*refs):
    local, first, landed, last = _exchange_copies(*refs)
    for arrival, onward in landed:
        arrival.wait_recv()
        onward.start()
    for cp in last:
        cp.wait_recv()
    for cp in first + [onward for _, onward in landed]:
        cp.wait_send()
    if local is not None:
        local.wait()


_EXCHANGE_SEMS = [pltpu.SemaphoreType.DMA((N_DEV - 1,)), pltpu.SemaphoreType.DMA((N_DEV - 1,)),
                  pltpu.SemaphoreType.DMA]


def pmatmul(a, b, dims, name, exchange=None):
    if dims == 'nn':
        (m, k), (k2, n) = a.shape, b.shape
    elif dims == 'nt':
        (m, k), (n, k2) = a.shape, b.shape
    else:
        (k, m), (k2, n) = a.shape, b.shape
    assert k == k2, (a.shape, b.shape, dims)
    single = k <= SINGLE_STEP_K
    tm = _pick(m, 512 if (single and dims == 'tn') else 768, LANE if dims == 'tn' else 8)
    tn = _pick(n, 1024, LANE)
    tk = k if single else _pick(k, 1024, LANE)
    nk = k // tk
    a_spec = pl.BlockSpec((tk, tm), lambda i, j, kk: (kk, i)) if dims == 'tn' else \
        pl.BlockSpec((tm, tk), lambda i, j, kk: (i, kk))
    b_spec = pl.BlockSpec((tn, tk), lambda i, j, kk: (j, kk)) if dims == 'nt' else \
        pl.BlockSpec((tk, tn), lambda i, j, kk: (kk, j))
    ni, nj = m // tm, n // tn
    acc = [] if single else [pltpu.VMEM((tm, tn), F32)]

    def matmul_step(a_ref, b_ref, o_ref, *acc_ref):
        if single:
            o_ref[...] = _dot(a_ref[...], b_ref[...], dims)
            return
        acc_ref, = acc_ref
        kk = pl.program_id(2)

        @pl.when(kk == 0)
        def _():
            acc_ref[...] = jnp.zeros_like(acc_ref)

        acc_ref[...] += _dot(a_ref[...], b_ref[...], dims)

        @pl.when(kk == nk - 1)
        def _():
            o_ref[...] = acc_ref[...]

    out_spec = pl.BlockSpec((tm, tn), lambda i, j, kk: (i, j))
    if exchange is None:
        return pl.pallas_call(
            matmul_step, name=name, grid=(ni, nj, nk),
            in_specs=[a_spec, b_spec], out_specs=out_spec, out_shape=jax.ShapeDtypeStruct((m, n), F32),
            scratch_shapes=acc, compiler_params=_params(("parallel", "parallel", "arbitrary")),
        )(a, b)

    kind, x = exchange

    def body(a_ref, b_ref, x_ref, o_ref, got_ref, *scratch):
        i, j, kk = pl.program_id(0), pl.program_id(1), pl.program_id(2)
        refs = (kind, x_ref, got_ref) + scratch[len(acc):]

        @pl.when(jnp.logical_and(jnp.logical_and(i == 0, j == 0), kk == 0))
        def _():
            _exchange_start(*refs)

        matmul_step(a_ref, b_ref, o_ref, *scratch[:len(acc)])

        @pl.when(jnp.logical_and(jnp.logical_and(i == ni - 1, j == nj - 1), kk == nk - 1))
        def _():
            _exchange_wait(*refs)

    any_spec = pl.BlockSpec(memory_space=pl.ANY)
    return pl.pallas_call(
        body, name=name, grid=(ni, nj, nk),
        in_specs=[a_spec, b_spec, any_spec], out_specs=[out_spec, any_spec],
        out_shape=[jax.ShapeDtypeStruct((m, n), F32),
                   jax.ShapeDtypeStruct(_exchange_out_shape(kind, x), x.dtype)],
        scratch_shapes=acc + _EXCHANGE_SEMS,
        compiler_params=_params(("arbitrary", "arbitrary", "arbitrary")),
    )(a, b, x)


def make_linear(name):
    @jax.custom_vjp
    def linear(a, w, w_proxy):
        return pmatmul(a, w, 'nn', name + '_fwd')

    def fwd(a, w, w_proxy):
        return pmatmul(a, w, 'nn', name + '_fwd'), (a, w)

    def bwd(res, g):
        a, w = res
        return pmatmul(g, w, 'nt', name + '_da'), jnp.zeros_like(w), pmatmul(a, g, 'tn', name + '_dw')

    linear.defvjp(fwd, bwd)
    return linear


def make_carrier_linear(name):
    def run(a, w, block, parts):
        out, gathered = pmatmul(a, w, 'nn', name + '_fwd', exchange=('gather', block))
        return out, gathered, jnp.zeros((N_DEV,) + parts.shape[1:], parts.dtype)

    @jax.custom_vjp
    def carrier(a, w, w_proxy, block, parts):
        return run(a, w, block, parts)

    def fwd(a, w, w_proxy, block, parts):
        return run(a, w, block, parts), (a, w, block)

    def bwd(res, cts):
        a, w, block = res
        g, _, ct = cts
        da, other_core = pmatmul(g, w, 'nt', name + '_da', exchange=('pair', ct))
        c = lax.axis_index("c")
        mine = lax.dynamic_index_in_dim(ct.reshape((N_DEV // 2, 2) + ct.shape[1:]), c, axis=1, keepdims=False)
        dw, parts = pmatmul(a, g, 'tn', name + '_dw',
                            exchange=('quad', add_halves_call(mine, other_core, name + '_add')))
        return da, jnp.zeros_like(w), dw, jnp.zeros_like(block), parts

    carrier.defvjp(fwd, bwd)
    return carrier


def make_rowwise(name, f, out_widths, tile, n_nondiff=0, f_diff=None):
    f_diff = f if f_diff is None else f_diff

    def fwd_call(rows, params):
        n = rows[0].shape[0]
        nr, npar = len(rows), len(params)

        def body(*refs):
            i = pl.program_id(0)
            outs = f(i, tuple(r[...] for r in refs[:nr]), tuple(p[...] for p in refs[nr:nr + npar]))
            for o_ref, val in zip(refs[nr + npar:], outs):
                o_ref[...] = val

        return pl.pallas_call(
            body, name=name + '_fwd', grid=(n // tile,),
            in_specs=[pl.BlockSpec((tile, r.shape[1]), lambda i: (i, 0)) for r in rows] +
                     [pl.BlockSpec(p.shape, lambda i: (0, 0)) for p in params],
            out_specs=[pl.BlockSpec((tile, w), lambda i: (i, 0)) for w in out_widths],
            out_shape=[jax.ShapeDtypeStruct((n, w), F32) for w in out_widths],
            compiler_params=_params(("parallel",)),
        )(*rows, *params)

    def bwd_call(rows, params, cots):
        n = rows[0].shape[0]
        nr, npar, nc = len(rows), len(params), len(cots)
        nd = nr - n_nondiff

        def body(*refs):
            i = pl.program_id(0)
            rv = tuple(r[...] for r in refs[:nr])
            pv = tuple(p[...] for p in refs[nr:nr + npar])
            cv = tuple(c[...] for c in refs[nr + npar:nr + npar + nc])
            out_refs = refs[nr + npar + nc:]
            _, vjp = jax.vjp(lambda dr, pp: tuple(f_diff(i, dr + rv[nd:], pp)), rv[:nd], pv)
            drows, dparams = vjp(cv)
            for o_ref, val in zip(out_refs[:nd], drows):
                o_ref[...] = val

            @pl.when(i == 0)
            def _():
                for o_ref in out_refs[nd:]:
                    o_ref[...] = jnp.zeros_like(o_ref)

            for o_ref, val in zip(out_refs[nd:], dparams):
                o_ref[...] += val

        outs = pl.pallas_call(
            body, name=name + '_bwd', grid=(n // tile,),
            in_specs=[pl.BlockSpec((tile, r.shape[1]), lambda i: (i, 0)) for r in rows] +
                     [pl.BlockSpec(p.shape, lambda i: (0, 0)) for p in params] +
                     [pl.BlockSpec((tile, c.shape[1]), lambda i: (i, 0)) for c in cots],
            out_specs=[pl.BlockSpec((tile, r.shape[1]), lambda i: (i, 0)) for r in rows[:nd]] +
                      [pl.BlockSpec(p.shape, lambda i: (0, 0)) for p in params],
            out_shape=[jax.ShapeDtypeStruct(r.shape, F32) for r in rows[:nd]] +
                      [jax.ShapeDtypeStruct(p.shape, F32) for p in params],
            compiler_params=_params(("arbitrary",)),
        )(*rows, *params, *cots)
        drows = tuple(outs[:nd]) + tuple(jnp.zeros_like(r) for r in rows[nd:])
        return drows, tuple(outs[nd:])

    @jax.custom_vjp
    def op(rows, params):
        return tuple(fwd_call(rows, params))

    def fwd(rows, params):
        return tuple(fwd_call(rows, params)), (rows, params)

    def bwd(res, cots):
        rows, params = res
        return bwd_call(rows, params, tuple(cots))

    op.defvjp(fwd, bwd)
    return op


def _silu(x):
    return x * jax.nn.sigmoid(x)


def _layer_norm(z, w, b):
    mu = jnp.mean(z, -1, keepdims=True)
    zc = z - mu
    var = jnp.mean(zc * zc, -1, keepdims=True)
    y = zc * lax.rsqrt(var + EPS)
    return y if w is None else y * w + b


def _softmax_rows(s):
    e = jnp.exp(s - jnp.max(s, -1, keepdims=True))
    return e / jnp.sum(e, -1, keepdims=True)


def attention_fwd_call(q, k, v, n_ctx):
    n = q.shape[0]
    tq = n_ctx
    grp = ATT_QH // ATT_KVH
    scale = ATT_D ** -0.5

    def body(q_ref, k_ref, v_ref, o_ref):
        def run(nk):
            p = _softmax_rows(_dot(q_ref[...], k_ref[0:nk, :], 'nt') * scale)
            o_ref[...] = _dot(p, v_ref[0:nk, :], 'nn')

        pl.when(pl.program_id(2) == 0)(lambda: run(n_ctx))
        pl.when(pl.program_id(2) > 0)(lambda: run(n))

    return pl.pallas_call(
        body, name='attn_fwd', grid=(ATT_KVH, grp, n // tq),
        in_specs=[pl.BlockSpec((tq, ATT_D), lambda h, g, i: (i, h * grp + g)),
                  pl.BlockSpec((n, ATT_D), lambda h, g, i: (0, h)),
                  pl.BlockSpec((n, ATT_D), lambda h, g, i: (0, h))],
        out_specs=pl.BlockSpec((tq, ATT_D), lambda h, g, i: (i, h * grp + g)),
        out_shape=jax.ShapeDtypeStruct((n, ATT_Q), F32),
        compiler_params=_params(("parallel", "parallel", "parallel")),
    )(q, k, v)


def attention_bwd_call(q, k, v, do, n_ctx):
    n = q.shape[0]
    tq = n_ctx
    grp = ATT_QH // ATT_KVH
    scale = ATT_D ** -0.5

    def body(q_ref, k_ref, v_ref, do_ref, dq_ref, dk_ref, dv_ref):
        @pl.when(jnp.logical_and(pl.program_id(1) == 0, pl.program_id(2) == 0))
        def _():
            dk_ref[...] = jnp.zeros_like(dk_ref)
            dv_ref[...] = jnp.zeros_like(dv_ref)

        def run(nk):
            qv, kv, vv, dov = q_ref[...], k_ref[0:nk, :], v_ref[0:nk, :], do_ref[...]
            p = _softmax_rows(_dot(qv, kv, 'nt') * scale)
            dv_ref[0:nk, :] += _dot(p, dov, 'tn')
            dp = _dot(dov, vv, 'nt')
            ds = p * (dp - jnp.sum(dp * p, -1, keepdims=True)) * scale
            dq_ref[...] = _dot(ds, kv, 'nn')
            dk_ref[0:nk, :] += _dot(ds, qv, 'tn')

        pl.when(pl.program_id(2) == 0)(lambda: run(n_ctx))
        pl.when(pl.program_id(2) > 0)(lambda: run(n))

    return pl.pallas_call(
        body, name='attn_bwd', grid=(ATT_KVH, grp, n // tq),
        in_specs=[pl.BlockSpec((tq, ATT_D), lambda h, g, i: (i, h * grp + g)),
                  pl.BlockSpec((n, ATT_D), lambda h, g, i: (0, h)),
                  pl.BlockSpec((n, ATT_D), lambda h, g, i: (0, h)),
                  pl.BlockSpec((tq, ATT_D), lambda h, g, i: (i, h * grp + g))],
        out_specs=[pl.BlockSpec((tq, ATT_D), lambda h, g, i: (i, h * grp + g)),
                   pl.BlockSpec((n, ATT_D), lambda h, g, i: (0, h)),
                   pl.BlockSpec((n, ATT_D), lambda h, g, i: (0, h))],
        out_shape=[jax.ShapeDtypeStruct((n, ATT_Q), F32), jax.ShapeDtypeStruct((n, ATT_KV), F32),
                   jax.ShapeDtypeStruct((n, ATT_KV), F32)],
        compiler_params=_params(("parallel", "arbitrary", "arbitrary")),
    )(q, k, v, do)


def make_attention(n_ctx):
    @jax.custom_vjp
    def attention(q, k, v):
        return attention_fwd_call(q, k, v, n_ctx)

    def fwd(q, k, v):
        return attention_fwd_call(q, k, v, n_ctx), (q, k, v)

    def bwd(res, do):
        return tuple(attention_bwd_call(*res, do, n_ctx))

    attention.defvjp(fwd, bwd)
    return attention


def _log_sigmoid(z):
    return jnp.minimum(z, 0.0) - jnp.log(1.0 + jnp.exp(-jnp.abs(z)))


def _gla_chunk(d, q, k, v, glr, wg, bg, state, mm, tri_dot):
    chunk = q.shape[0]
    r = lax.broadcasted_iota(jnp.int32, (chunk, chunk), 0)
    c = lax.broadcasted_iota(jnp.int32, (chunk, chunk), 1)
    lower = jnp.where(c <= r, 1.0, 0.0).astype(F32)
    upper = jnp.where(c >= r, 1.0, 0.0).astype(F32)
    tri = jnp.where(d == 0, lower, upper)
    tri_t = jnp.where(d == 0, upper, lower)
    log_a = _log_sigmoid(mm(glr, wg, 'nn') + bg) * (1.0 / GLA_TAU)
    outs, new_state = [], []
    for h in range(GLA_HEADS):
        la = log_a[:, h * GLA_DK:(h + 1) * GLA_DK]
        b = tri_dot(tri, tri_t, la)
        tot = jnp.sum(la, axis=0, keepdims=True)
        qh = q[:, h * GLA_DK:(h + 1) * GLA_DK] * (GLA_DK ** -0.5)
        kh = k[:, h * GLA_DK:(h + 1) * GLA_DK]
        vh = v[:, h * GLA_DV:(h + 1) * GLA_DV]
        qe = qh * jnp.exp(b)
        ke = kh * jnp.exp(-b)
        att = mm(qe, ke, 'nt') * tri
        outs.append(mm(att, vh, 'nn') + mm(qe, state[h], 'nn'))
        kd = kh * jnp.exp(tot - b)
        decay = jnp.transpose(jnp.broadcast_to(jnp.exp(tot), (GLA_DK, GLA_DK)))
        decay = jnp.concatenate([decay] * (GLA_DV // GLA_DK), axis=1)
        new_state.append(decay * state[h] + mm(kd, vh, 'tn'))
    return jnp.concatenate(outs, axis=1), tuple(new_state)


def _gla_chunk_of(d, s, nc, ns):
    falling = jnp.where(s < nc, nc - 1 - s, ns - 1 - (s - nc))
    return jnp.where(d == 0, s, falling)


def gla_fwd_call(q, k, v, glr, wg, bg, n_ctx):
    n = q.shape[0]
    ch = GLA_CHUNK
    ns, nc = n // ch, n_ctx // ch

    def at(d, s):
        return _gla_chunk_of(d, s, nc, ns)

    def body(q_ref, k_ref, v_ref, glr_ref, wg_ref, bg_ref, o_ref, saved_ref, state_ref):
        d, s = pl.program_id(0), pl.program_id(1)

        @pl.when(s == 0)
        def _():
            state_ref[...] = jnp.zeros_like(state_ref)

        saved_ref[...] = state_ref[...]
        state = tuple(state_ref[h] for h in range(GLA_HEADS))
        o, new_state = _gla_chunk(d, q_ref[...], k_ref[...], v_ref[...], glr_ref[...], wg_ref[...],
                                  bg_ref[...], state, _dot, _tri_dot_plain)
        o_ref[...] = o
        for h in range(GLA_HEADS):
            state_ref[h] = new_state[h]

    return pl.pallas_call(
        body, name='gla_fwd', grid=(2, ns),
        in_specs=[pl.BlockSpec((ch, GLA_QK), lambda d, s: (at(d, s), 0)),
                  pl.BlockSpec((ch, GLA_QK), lambda d, s: (at(d, s), 0)),
                  pl.BlockSpec((ch, GLA_V), lambda d, s: (at(d, s), 0)),
                  pl.BlockSpec((ch, LANE), lambda d, s: (at(d, s), 0)),
                  pl.BlockSpec((None, LANE, GLA_QK), lambda d, s: (d, 0, 0)),
                  pl.BlockSpec((None, 1, GLA_QK), lambda d, s: (d, 0, 0))],
        out_specs=[pl.BlockSpec((None, ch, GLA_V), lambda d, s: (d, at(d, s), 0)),
                   pl.BlockSpec((None, None, GLA_HEADS, GLA_DK, GLA_DV), lambda d, s: (d, s, 0, 0, 0))],
        out_shape=[jax.ShapeDtypeStruct((2, n, GLA_V), F32),
                   jax.ShapeDtypeStruct((2, ns, GLA_HEADS, GLA_DK, GLA_DV), F32)],
        scratch_shapes=[pltpu.VMEM((GLA_HEADS, GLA_DK, GLA_DV), F32)],
        compiler_params=_params(("parallel", "arbitrary")),
    )(q, k, v, glr, wg, bg)


def gla_bwd_call(q, k, v, glr, wg, bg, saved, do, n_ctx):
    n = q.shape[0]
    ch = GLA_CHUNK
    ns, nc = n // ch, n_ctx // ch

    def at(d, s):
        return _gla_chunk_of(d, ns - 1 - s, nc, ns)

    def body(q_ref, k_ref, v_ref, glr_ref, wg_ref, bg_ref, saved_ref, do_ref,
             dq_ref, dk_ref, dv_ref, dglr_ref, dwg_ref, dbg_ref, dstate_ref):
        d, s = pl.program_id(0), pl.program_id(1)

        @pl.when(s == 0)
        def _():
            dstate_ref[...] = jnp.zeros_like(dstate_ref)
            dwg_ref[...] = jnp.zeros_like(dwg_ref)
            dbg_ref[...] = jnp.zeros_like(dbg_ref)

        state = tuple(saved_ref[h] for h in range(GLA_HEADS))
        dstate = tuple(dstate_ref[h] for h in range(GLA_HEADS))

        def f(qv, kv, vv, glrv, wgv, bgv, st):
            return _gla_chunk(d, qv, kv, vv, glrv, wgv, bgv, st, _mm, _tri_dot)

        _, vjp = jax.vjp(f, q_ref[...], k_ref[...], v_ref[...], glr_ref[...], wg_ref[...], bg_ref[...], state)
        dq, dk, dv, dglr, dwg, dbg, dprev = vjp((do_ref[...], dstate))
        dq_ref[...] = dq
        dk_ref[...] = dk
        dv_ref[...] = dv
        dglr_ref[...] = dglr
        dwg_ref[...] += dwg
        dbg_ref[...] += dbg
        for h in range(GLA_HEADS):
            dstate_ref[h] = dprev[h]

    return pl.pallas_call(
        body, name='gla_bwd', grid=(2, ns),
        in_specs=[pl.BlockSpec((ch, GLA_QK), lambda d, s: (at(d, s), 0)),
                  pl.BlockSpec((ch, GLA_QK), lambda d, s: (at(d, s), 0)),
                  pl.BlockSpec((ch, GLA_V), lambda d, s: (at(d, s), 0)),
                  pl.BlockSpec((ch, LANE), lambda d, s: (at(d, s), 0)),
                  pl.BlockSpec((None, LANE, GLA_QK), lambda d, s: (d, 0, 0)),
                  pl.BlockSpec((None, 1, GLA_QK), lambda d, s: (d, 0, 0)),
                  pl.BlockSpec((None, None, GLA_HEADS, GLA_DK, GLA_DV), lambda d, s: (d, ns - 1 - s, 0, 0, 0)),
                  pl.BlockSpec((None, ch, GLA_V), lambda d, s: (d, at(d, s), 0))],
        out_specs=[pl.BlockSpec((None, ch, GLA_QK), lambda d, s: (d, at(d, s), 0)),
                   pl.BlockSpec((None, ch, GLA_QK), lambda d, s: (d, at(d, s), 0)),
                   pl.BlockSpec((None, ch, GLA_V), lambda d, s: (d, at(d, s), 0)),
                   pl.BlockSpec((None, ch, LANE), lambda d, s: (d, at(d, s), 0)),
                   pl.BlockSpec((None, LANE, GLA_QK), lambda d, s: (d, 0, 0)),
                   pl.BlockSpec((None, 1, GLA_QK), lambda d, s: (d, 0, 0))],
        out_shape=[jax.ShapeDtypeStruct((2, n, GLA_QK), F32), jax.ShapeDtypeStruct((2, n, GLA_QK), F32),
                   jax.ShapeDtypeStruct((2, n, GLA_V), F32), jax.ShapeDtypeStruct((2, n, LANE), F32),
                   jax.ShapeDtypeStruct((2, LANE, GLA_QK), F32), jax.ShapeDtypeStruct((2, 1, GLA_QK), F32)],
        scratch_shapes=[pltpu.VMEM((GLA_HEADS, GLA_DK, GLA_DV), F32)],
        compiler_params=_params(("parallel", "arbitrary")),
    )(q, k, v, glr, wg, bg, saved, do)


def make_gla(n_ctx):
    @jax.custom_vjp
    def gla(q, k, v, glr, wg, bg):
        return gla_fwd_call(q, k, v, glr, wg, bg, n_ctx)[0]

    def fwd(q, k, v, glr, wg, bg):
        o, saved = gla_fwd_call(q, k, v, glr, wg, bg, n_ctx)
        return o, (q, k, v, glr, wg, bg, saved)

    def bwd(res, do):
        dq, dk, dv, dglr, dwg, dbg = gla_bwd_call(*res, do, n_ctx)
        return dq[0] + dq[1], dk[0] + dk[1], dv[0] + dv[1], dglr[0] + dglr[1], dwg, dbg

    gla.defvjp(fwd, bwd)
    return gla


def _s5_chunk_at(kind, g, nch):
    if kind == 0:
        return g
    if kind == 1:
        return jnp.where(g == 0, 0, nch - g)
    if kind == 2:
        return nch - 1 - g
    return jnp.where(g == nch - 1, 0, g + 1)


def s5_scan_call(xr, xi, ar, ai, kind, lc, name):
    nch = xr.shape[0] // lc
    rising = kind in (0, 3)

    def body(xr_ref, xi_ref, ar_ref, ai_ref, sr_ref, si_ref, st_ref):
        g = pl.program_id(0)

        @pl.when(g == 0)
        def _():
            st_ref[...] = jnp.zeros_like(st_ref)

        a_r, a_i = ar_ref[...], ai_ref[...]

        def step(j, carry):
            cr, ci = carry
            tt = j if rising else lc - 1 - j
            nr = a_r * cr - a_i * ci + xr_ref[tt]
            ni = a_r * ci + a_i * cr + xi_ref[tt]
            sr_ref[tt] = nr
            si_ref[tt] = ni
            return nr, ni

        cr, ci = lax.fori_loop(0, lc, step, (st_ref[0], st_ref[1]), unroll=8)
        st_ref[0] = cr
        st_ref[1] = ci

    blk = pl.BlockSpec((lc, S5_ROWS, LANE), lambda g: (_s5_chunk_at(kind, g, nch), 0, 0))
    par = pl.BlockSpec((S5_ROWS, LANE), lambda g: (0, 0))
    return pl.pallas_call(
        body, name=name, grid=(nch,), in_specs=[blk, blk, par, par], out_specs=[blk, blk],
        scratch_shapes=[pltpu.VMEM((2, S5_ROWS, LANE), F32)],
        out_shape=[jax.ShapeDtypeStruct(xr.shape, F32), jax.ShapeDtypeStruct(xr.shape, F32)],
        compiler_params=_params(("arbitrary",)),
    )(xr, xi, ar, ai)


def s5_da_call(lr, li, pr, pi, lc):
    t = lr.shape[0]

    def body(lr_ref, li_ref, pr_ref, pi_ref, dar_ref, dai_ref):
        @pl.when(pl.program_id(0) == 0)
        def _():
            dar_ref[...] = jnp.zeros_like(dar_ref)
            dai_ref[...] = jnp.zeros_like(dai_ref)

        a, b, c, e = lr_ref[...], li_ref[...], pr_ref[...], pi_ref[...]
        dar_ref[...] += jnp.sum(a * c + b * e, axis=0)
        dai_ref[...] += jnp.sum(b * c - a * e, axis=0)

    blk = pl.BlockSpec((lc, S5_ROWS, LANE), lambda g: (g, 0, 0))
    par = pl.BlockSpec((S5_ROWS, LANE), lambda g: (0, 0))
    return pl.pallas_call(
        body, name='s5_da', grid=(t // lc,), in_specs=[blk] * 4, out_specs=[par, par],
        out_shape=[jax.ShapeDtypeStruct((S5_ROWS, LANE), F32)] * 2,
        compiler_params=_params(("arbitrary",)),
    )(lr, li, pr, pi)


def make_s5_scan(direction, n_chunks, lc):
    def predecessors(s):
        zero = jnp.zeros((1,) + s.shape[1:], F32)
        if direction == 0:
            return jnp.concatenate([zero, s[:-1]])
        return jnp.concatenate([s[1:lc], zero, s[lc + 1:], s[0:1]])

    @jax.custom_vjp
    def scan(xr, xi, ar, ai):
        return tuple(s5_scan_call(xr, xi, ar, ai, direction, lc, 's5_scan_d%d' % direction))

    def fwd(xr, xi, ar, ai):
        sr, si = s5_scan_call(xr, xi, ar, ai, direction, lc, 's5_scan_d%d' % direction)
        return (sr, si), (sr, si, ar, ai)

    def bwd(res, g):
        sr, si, ar, ai = res
        lr, li = s5_scan_call(g[0], g[1], ar, -ai, 2 + direction, lc, 's5_adjoint_d%d' % direction)
        dar, dai = s5_da_call(lr, li, predecessors(sr), predecessors(si), lc)
        return lr, li, dar, dai

    scan.defvjp(fwd, bwd)
    return scan


def loss_fwd_call(y, target, tile):
    n, dm = y.shape

    def body(y_ref, t_ref, o_ref):
        e = y_ref[...] - t_ref[...]
        o_ref[...] = jnp.full(o_ref.shape, 0.5 * jnp.sum(e * e) / dm, F32)

    out = pl.pallas_call(
        body, name='loss_fwd', grid=(n // tile,),
        in_specs=[pl.BlockSpec((tile, dm), lambda i: (i, 0))] * 2,
        out_specs=pl.BlockSpec((8, LANE), lambda i: (i, 0)),
        out_shape=jax.ShapeDtypeStruct((8 * (n // tile), LANE), F32),
        compiler_params=_params(("parallel",)),
    )(y, target)
    return jnp.sum(out[::8, 0])


def make_loss(tile):
    @jax.custom_vjp
    def loss(y, target):
        return loss_fwd_call(y, target, tile)

    def fwd(y, target):
        return loss_fwd_call(y, target, tile), (y, target)

    def bwd(res, g):
        y, target = res
        scale = jnp.full((1, y.shape[1]), g / y.shape[1], F32)

        def body(y_ref, t_ref, s_ref, o_ref):
            o_ref[...] = (y_ref[...] - t_ref[...]) * s_ref[...]

        dy = pl.pallas_call(
            body, name='loss_bwd', grid=(y.shape[0] // tile,),
            in_specs=[pl.BlockSpec((tile, y.shape[1]), lambda i: (i, 0))] * 2 +
                     [pl.BlockSpec((1, y.shape[1]), lambda i: (0, 0))],
            out_specs=pl.BlockSpec((tile, y.shape[1]), lambda i: (i, 0)),
            out_shape=jax.ShapeDtypeStruct(y.shape, F32),
            compiler_params=_params(("parallel",)),
        )(y, target, scale)
        return dy, jnp.zeros_like(target)

    loss.defvjp(fwd, bwd)
    return loss


def all_gather(x, name):
    return exchange_call('gather', x, name)


def all_to_all(x, name):
    return exchange_call('scatter', x, name)


def exchange_call(kind, x, name):
    def body(x_ref, out_ref, send_sems, recv_sems, local_sem):
        refs = (kind, x_ref, out_ref, send_sems, recv_sems, local_sem)
        _exchange_start(*refs)
        _exchange_wait(*refs)

    return pl.pallas_call(
        body, name=name, out_shape=jax.ShapeDtypeStruct(_exchange_out_shape(kind, x), x.dtype),
        in_specs=[pl.BlockSpec(memory_space=pl.ANY)], out_specs=pl.BlockSpec(memory_space=pl.ANY),
        scratch_shapes=_EXCHANGE_SEMS,
    )(x)


def add_halves_call(a, b, name):
    p, r, c = a.shape
    tr = _pick(r, 256, 16)

    def body(a_ref, b_ref, o_ref):
        o_ref[...] = (a_ref[...].astype(F32) + b_ref[...].astype(F32)).astype(o_ref.dtype)

    blk = pl.BlockSpec((None, tr, c), lambda q, i: (q, i, 0))
    return pl.pallas_call(
        body, name=name, grid=(p, r // tr), in_specs=[blk, blk], out_specs=blk,
        out_shape=jax.ShapeDtypeStruct(a.shape, BF16), compiler_params=_params(("parallel", "parallel")),
    )(a, b)


def reduce_scatter_parts(g, name):
    c = lax.axis_index("c")
    mine = lax.dynamic_index_in_dim(g.reshape((N_DEV // 2, 2) + g.shape[1:]), c, axis=1, keepdims=False)
    return exchange_call('quad', add_halves_call(mine, exchange_call('pair', g, name + '_pair'), name + '_add'),
                         name + '_quad')


def sum_parts_call(parts, name):
    p, r, c = parts.shape
    tr = _pick(r, 256, 8)

    def body(p_ref, o_ref):
        acc = p_ref[0].astype(F32)
        for j in range(1, p):
            acc = acc + p_ref[j].astype(F32)
        o_ref[...] = acc

    return pl.pallas_call(
        body, name=name, grid=(r // tr,), in_specs=[pl.BlockSpec((p, tr, c), lambda i: (0, i, 0))],
        out_specs=pl.BlockSpec((tr, c), lambda i: (i, 0)), out_shape=jax.ShapeDtypeStruct((r, c), F32),
        compiler_params=_params(("parallel",)),
    )(parts)


@jax.custom_vjp
def gather_rows(x):
    return all_gather(x, 'mod_gather')


def _gather_rows_fwd(x):
    return all_gather(x, 'mod_gather'), None


def _gather_rows_bwd(_, ct):
    return (sum_parts_call(all_to_all(ct, 'mod_scatter'), 'mod_scatter_sum'),)


gather_rows.defvjp(_gather_rows_fwd, _gather_rows_bwd)


def adamw_call(parts, w, m, v, name):
    p, r, c = parts.shape
    tr = _pick(r, 128, 8)
    c1 = 1.0 / (1.0 - ADAM_B1 ** ADAM_STEP)
    c2 = 1.0 / (1.0 - ADAM_B2 ** ADAM_STEP)

    def body(p_ref, w_ref, m_ref, v_ref, g_ref, d_ref, m2_ref, v2_ref):
        g = p_ref[0].astype(F32)
        for j in range(1, p):
            g = g + p_ref[j].astype(F32)
        m2 = ADAM_B1 * m_ref[...] + (1.0 - ADAM_B1) * g
        v2 = ADAM_B2 * v_ref[...] + (1.0 - ADAM_B2) * (g * g)
        g_ref[...] = g
        m2_ref[...] = m2
        v2_ref[...] = v2
        d_ref[...] = -ADAM_LR * ((m2 * c1) / (jnp.sqrt(v2 * c2) + ADAM_EPS) + ADAM_WD * w_ref[...])

    blk = pl.BlockSpec((tr, c), lambda i: (i, 0))
    return pl.pallas_call(
        body, name=name, grid=(r // tr,),
        in_specs=[pl.BlockSpec((p, tr, c), lambda i: (0, i, 0)), blk, blk, blk], out_specs=[blk] * 4,
        out_shape=[jax.ShapeDtypeStruct((r, c), F32)] * 4,
        compiler_params=_params(("parallel",)),
    )(parts, w, m, v)


def _deinterleave(z, heads):
    lead = z.shape[:-1]
    return z.reshape(lead + (heads, ATT_D // 2, 2)).swapaxes(-1, -2).reshape(lead + (heads * ATT_D,))


def _unshard_cols(g):
    return jnp.moveaxis(g, 0, -2).reshape(g.shape[1:-1] + (N_DEV * g.shape[-1],))


def _pack_w_in(g, dm):
    w = _unshard_cols(g)
    splits = (GLA_QK, GLA_QK, GLA_V, GLA_V, GLA_RANK, S5_WIDTH, ATT_Q, ATT_KV, ATT_KV, 3 * dm)
    gq, gk, gv, gr, glr, su, aq, ak, av, bg = jnp.split(w, np.cumsum(splits)[:-1].tolist(), axis=-1)
    pad = jnp.zeros(w.shape[:-1] + (LANE - GLA_RANK,), w.dtype)
    return jnp.concatenate([gq, gk, gv, gr, su, _deinterleave(aq, ATT_QH), _deinterleave(ak, ATT_KVH),
                            av, bg, glr, pad], axis=-1)


def _in_widths(dm):
    return (GLA_QK, GLA_QK, GLA_V, GLA_V, S5_WIDTH, ATT_Q, ATT_KV, ATT_KV, 3 * dm, LANE)


def _make_split(widths):
    cuts = np.cumsum(widths)[:-1].tolist()

    @jax.custom_vjp
    def split(z):
        return tuple(jnp.split(z, cuts, axis=1))

    def fwd(z):
        return tuple(jnp.split(z, cuts, axis=1)), None

    def bwd(_, cots):
        return (jnp.concatenate(cots, axis=1),)

    split.defvjp(fwd, bwd)
    return split


def _block_diag_in(b):
    eye = jnp.eye(S5_GROUPS, dtype=F32)
    return jnp.einsum('gpc,gh->gchp', b, eye).reshape(S5_WIDTH, S5_CH)


def _block_diag_out(cm):
    eye = jnp.eye(S5_GROUPS, dtype=F32)
    return jnp.einsum('gcp,gh->gphc', cm, eye).reshape(S5_CH, S5_WIDTH)


def _s5_discretise(lam_re, lam_im, log_dt):
    dt = jnp.exp(log_dt)[:, None]
    mag = jnp.exp(lam_re * dt)
    a_re, a_im = mag * jnp.cos(lam_im * dt), mag * jnp.sin(lam_im * dt)
    den = lam_re * lam_re + lam_im * lam_im
    nr, ni = a_re - 1, a_im
    k_re = (nr * lam_re + ni * lam_im) / den
    k_im = (ni * lam_re - nr * lam_im) / den
    return a_re, a_im, k_re, k_im


def _rope_tables(n_ctx, n_lat):
    rows = jnp.repeat(jnp.arange(n_lat // GRID_W), GRID_W).astype(F32)
    cols = jnp.tile(jnp.arange(GRID_W), n_lat // GRID_W).astype(F32)
    n_freq = ATT_D // 4
    inv = ROPE_THETA ** (-jnp.arange(n_freq, dtype=F32) / n_freq)
    ang = jnp.concatenate([rows[:, None] * inv, cols[:, None] * inv], -1)
    cos, sin = jnp.cos(ang), jnp.sin(ang)
    cosf = jnp.concatenate([jnp.ones((n_ctx, ATT_D), F32), jnp.concatenate([cos, cos], -1)], 0)
    sinf = jnp.concatenate([jnp.zeros((n_ctx, ATT_D), F32), jnp.concatenate([-sin, sin], -1)], 0)
    return cosf, sinf


def _build_ops(n, n_ctx, dm, dff, depth):
    t256 = min(256, n_ctx)
    t64 = 64
    alpha = (2 * depth) ** 0.25
    ops = {}

    def seg(i, tile, pc, pl_):
        return jnp.where(i < n_ctx // tile, pc, pl_)

    def modulate_f(i, rows, params):
        sh = seg(i, t256, params[0], params[1])
        sc = seg(i, t256, params[2], params[3])
        return (rows[0] * (1.0 + sc) + sh,)

    def postnorm_f(i, rows, params):
        g = seg(i, t256, params[0], params[1])
        return (_layer_norm(alpha * rows[0] + g * rows[1], params[2], params[3]),)

    def gla_out_f(i, rows, params):
        o = rows[0] + rows[1]
        heads = [_layer_norm(o[:, h * GLA_DV:(h + 1) * GLA_DV], None, None) for h in range(GLA_HEADS)]
        return (jnp.concatenate(heads, axis=1) * params[0] * _silu(rows[2]),)

    def s5_in_f(i, rows, params):
        br, bi = rows[0][:, :S5_CH], rows[0][:, S5_CH:]
        kr0, ki0, kr1, ki1 = params
        return (kr0 * br - ki0 * bi, kr0 * bi + ki0 * br, kr1 * br - ki1 * bi, kr1 * bi + ki1 * br)

    def s5_mid_f(i, rows, params):
        return (jax.nn.gelu(rows[0] + rows[1] * params[0]),)

    def s5_gate_f(i, rows, params):
        return (rows[0] * jax.nn.sigmoid(rows[1]),)

    def qk_f(swap):
        def f(i, rows, params):
            aq, ak, cosf, sinf = rows
            outs = []
            for z, w, heads in ((aq, params[0], ATT_QH), (ak, params[1], ATT_KVH)):
                hs = []
                for h in range(heads):
                    zh = z[:, h * ATT_D:(h + 1) * ATT_D]
                    zn = zh * lax.rsqrt(jnp.mean(zh * zh, -1, keepdims=True) + EPS) * w
                    hs.append(zn * cosf + swap(zn) * sinf)
                outs.append(jnp.concatenate(hs, axis=1))
            return tuple(outs)
        return f

    def silu_f(i, rows, params):
        return (_silu(rows[0]),)

    def merge_f(i, rows, params):
        pg, ps, pa, bg = rows
        gate = jax.nn.sigmoid(bg)
        return (gate[:, :dm] * pg + gate[:, dm:2 * dm] * ps + gate[:, 2 * dm:] * pa,)

    def swiglu_f(i, rows, params):
        return (_silu(rows[0]) * rows[1],)

    ops['modulate1'] = make_rowwise('modulate1', modulate_f, (dm,), t256)
    ops['modulate2'] = make_rowwise('modulate2', modulate_f, (dm,), t256)
    ops['postnorm1'] = make_rowwise('postnorm1', postnorm_f, (dm,), t256)
    ops['postnorm2'] = make_rowwise('postnorm2', postnorm_f, (dm,), t256)
    ops['gla_out'] = make_rowwise('gla_out', gla_out_f, (GLA_V,), t256)
    ops['s5_in'] = make_rowwise('s5_in', s5_in_f, (S5_CH,) * 4, t64)
    ops['s5_mid'] = make_rowwise('s5_mid', s5_mid_f, (S5_WIDTH,), t256)
    ops['s5_gate'] = make_rowwise('s5_gate', s5_gate_f, (S5_WIDTH,), t256)
    ops['merge'] = make_rowwise('merge', merge_f, (dm,), t64)
    ops['swiglu'] = make_rowwise('swiglu', swiglu_f, (dff,), t64)
    ops['qk_prep'] = make_rowwise('qk_prep', qk_f(_swap_plain), (ATT_Q, ATT_KV), t256, n_nondiff=2,
                                  f_diff=qk_f(_swap_halves))
    ops['silu_c'] = make_rowwise('silu_c', silu_f, (dm,), 16)
    return ops


def _flat2d(a):
    return a.reshape(-1, a.shape[-1])


def _step(given):
    x, c, ctx = given['x'][0], given['c'], given['ctx'][0]
    target = given['loss_target'][0]
    n_lat, dm = x.shape
    n_ctx = ctx.shape[0]
    n = n_ctx + n_lat
    depth = given['w_ada'].shape[0]
    dff = given['w_ffn_out'].shape[1] * N_DEV
    ada_cols = given['w_ada'].shape[2]
    xi, yi, ci = _coords()
    me = 4 * xi + 2 * yi + ci
    lc = n_ctx
    ops = _build_ops(n, n_ctx, dm, dff, depth)
    names = ['ada', 'in', 's5b', 's5c0', 's5c1', 's5c2', 's5c3', 'glu', 'pg', 'ps', 'pa', 'out', 'fa', 'fb', 'fo']
    lin = {k: make_linear('lin_' + k) for k in names}
    split_in = _make_split(_in_widths(dm))
    gla = make_gla(n_ctx)
    attention = make_attention(n_ctx)
    scans = [make_s5_scan(d, n // lc, lc) for d in range(2)]
    loss_op = make_loss(min(256, n_lat))
    cosf, sinf = _rope_tables(n_ctx, n_lat)

    carried_by = {'w_in': 'in', 'w_ffn_in': 'fa', 'w_ffn_out': 'fo', 'w_out': 'out', 'w_proj_gla': 'pg',
                  'w_proj_s5': 'ps', 'w_proj_attn': 'pa', 'w_s5_glu': 'glu'}
    carrier = {k: make_carrier_linear('lin_' + k) for k in carried_by.values()}
    blocks = [{k: given[k][l].astype(BF16) for k in FAMILIES} for l in range(depth)]
    gathered0 = {k: all_gather(blocks[0][k], 'gather_' + k) for k in FAMILIES}
    gates = {k: all_gather(given[k], 'gather_' + k) for k in ('w_gla_gate', 'b_gla_gate')}
    c_all = all_gather(c, 'gather_c').reshape(N_DEV, dm)

    diff0 = {
        'x': x,
        'rep': {k: given[k] for k in REPLICATED},
        'grads': [{k: jnp.zeros((N_DEV if l == 0 else N_DEV // 2,) + blocks[l][k].shape, BF16) for k in FAMILIES}
                  for l in range(depth)],
        'gates': {k: jnp.zeros((N_DEV,) + given[k].shape, F32) for k in gates},
        'ada': jnp.zeros(given['w_ada'].shape, F32),
    }
    w_ada_bf = given['w_ada'].astype(BF16)

    def unpack(g):
        w_fi = _unshard_cols(g['w_ffn_in'])
        rows = lambda z: z.reshape((N_DEV * z.shape[1],) + z.shape[2:])
        return dict(w_in=_pack_w_in(g['w_in'], dm), w_fa=w_fi[:, :dff], w_fb=w_fi[:, dff:],
                    w_fo=rows(g['w_ffn_out']), w_out=rows(g['w_out']), w_pg=_unshard_cols(g['w_proj_gla']),
                    w_ps=_unshard_cols(g['w_proj_s5']), w_pa=_unshard_cols(g['w_proj_attn']),
                    w_glu=rows(g['w_s5_glu']))

    def loss_fn(diff):
        r = diff['rep']
        wgate = _unshard_cols(gates['w_gla_gate'] + diff['gates']['w_gla_gate'])
        wgate = jnp.pad(wgate, ((0, 0), (0, 0), (0, LANE - GLA_RANK), (0, 0)))
        bgate = _unshard_cols(gates['b_gla_gate'] + diff['gates']['b_gla_gate'])[:, :, None, :]

        c16 = jnp.concatenate([c_all, r['c_ctx'][None], jnp.zeros((7, dm), F32)], 0)
        c16s, = ops['silu_c']((c16,), ())

        def small(l):
            disc = jax.vmap(_s5_discretise)(r['s5_lam_re'][l], r['s5_lam_im'][l], r['s5_log_dt'][l])
            a_re, a_im = (z.reshape(2, S5_ROWS, LANE) for z in disc[:2])
            k_re, k_im = (z.reshape(2, 1, S5_CH) for z in disc[2:])
            b_full = jnp.concatenate([_block_diag_in(r['s5_b_re'][l]), _block_diag_in(r['s5_b_im'][l])], -1)
            c_re, c_im = r['s5_c_re'][l], r['s5_c_im'][l]
            c_full = [_block_diag_out(c_re[0]), -_block_diag_out(c_im[0]),
                      _block_diag_out(c_re[1]), -_block_diag_out(c_im[1])]
            row = lambda k: r[k][l][None, :]
            return dict(w_ada=w_ada_bf[l], p_ada=diff['ada'][l], b_ada=r['b_ada'][l], wgate=wgate[l], bgate=bgate[l],
                        a_re=a_re, a_im=a_im, k_re=k_re, k_im=k_im, b_full=b_full, c_full=c_full,
                        qw=_deinterleave(r['q_norm_w'][l], 1)[None, :], kw=_deinterleave(r['k_norm_w'][l], 1)[None, :],
                        gnw=row('gla_norm_w'), s5_d=row('s5_d'), ln1_w=row('ln1_w'), ln1_b=row('ln1_b'),
                        ln2_w=row('ln2_w'), ln2_b=row('ln2_b'))

        family_of = {v: f for f, v in carried_by.items()}

        def matmul(k, a, p, nxt, got):
            if nxt is None or k not in family_of:
                return lin[k](a, p['w_' + k], p['p_' + k])
            fam = family_of[k]
            out, got['w'][fam], got['g'][fam] = carrier[k](a, p['w_' + k], p['p_' + k], nxt[0][fam], nxt[1][fam])
            return out

        def layer(xc, p, nxt, got):
            msh = lin['ada'](c16s, p['w_ada'], p['p_ada']) + lax.dynamic_slice(p['b_ada'], (me * ada_cols,), (ada_cols,))
            mods = gather_rows(msh)
            m_lat = lax.dynamic_index_in_dim(mods, me, axis=1, keepdims=False).reshape(1, 6 * dm)
            m_ctx = mods[:, N_DEV, :].reshape(1, 6 * dm)
            sh1, sc1, g1, sh2, sc2, g2 = [(m_ctx[:, j * dm:(j + 1) * dm], m_lat[:, j * dm:(j + 1) * dm]) for j in range(6)]

            h, = ops['modulate1']((xc,), sh1 + sc1)
            gq, gk, gv, gr, su, aq, ak, av, bgt, glr = split_in(matmul('in', h, p, nxt, got))
            o2 = gla(gq, gk, gv, glr, p['wgate'], p['bgate'])
            o_gla, = ops['gla_out']((o2[0], o2[1], gr), (p['gnw'],))

            bu = lin['s5b'](su, p['b_full'].astype(BF16), p['b_full'])
            xin = ops['s5_in']((bu,), (p['k_re'][0], p['k_im'][0], p['k_re'][1], p['k_im'][1]))
            tile3 = lambda z: z.reshape(n, S5_ROWS, LANE)
            st = []
            for d in range(2):
                st += scans[d](tile3(xin[2 * d]), tile3(xin[2 * d + 1]), p['a_re'][d], p['a_im'][d])
            y0 = sum(lin['s5c%d' % j](z.reshape(n, S5_CH), cm.astype(BF16), cm)
                     for j, (z, cm) in enumerate(zip(st, p['c_full'])))
            ya, = ops['s5_mid']((y0, su), (p['s5_d'],))
            o_s5, = ops['s5_gate']((ya, matmul('glu', ya, p, nxt, got)), ())

            q, k = ops['qk_prep']((aq, ak, cosf, sinf), (p['qw'], p['kw']))
            o_att = attention(q, k, av)

            merged, = ops['merge']((matmul('pg', o_gla, p, nxt, got), matmul('ps', o_s5, p, nxt, got),
                                    matmul('pa', o_att, p, nxt, got), bgt), ())
            mix = matmul('out', merged, p, nxt, got)
            x1, = ops['postnorm1']((xc, mix), g1 + (p['ln1_w'], p['ln1_b']))
            h2, = ops['modulate2']((x1,), sh2 + sc2)
            act, = ops['swiglu']((matmul('fa', h2, p, nxt, got), lin['fb'](h2, p['w_fb'], p['p_fb'])), ())
            x2, = ops['postnorm2']((x1, matmul('fo', act, p, nxt, got)), g2 + (p['ln2_w'], p['ln2_b']))
            return x2

        xc = jnp.concatenate([ctx, diff['x']], 0)
        weights, stand_ins = gathered0, diff['grads'][0]
        for l in range(depth):
            p = small(l)
            p.update(unpack(weights))
            p.update({'p' + k[1:]: v for k, v in unpack({k: v.astype(F32) for k, v in stand_ins.items()}).items()})
            nxt = (blocks[l + 1], diff['grads'][l + 1]) if l + 1 < depth else None
            got = {'w': {}, 'g': {}}
            xc = layer(xc, p, nxt, got)
            weights, stand_ins = {k: lax.stop_gradient(v) for k, v in got['w'].items()}, got['g']
        return loss_op(xc[n_ctx:], target)

    loss_local, grads = jax.value_and_grad(loss_fn)(diff0)
    loss = lax.psum(loss_local, ("x", "y", "c"))

    out = {}

    def update(k, parts):
        w2 = _flat2d(given[k])
        res = adamw_call(parts.reshape((parts.shape[0],) + w2.shape), w2, _flat2d(given['m_' + k]),
                         _flat2d(given['v_' + k]), 'adamw_' + k)
        out[k] = [z.reshape(given[k].shape) for z in res]

    parts = {k: jnp.stack([reduce_scatter_parts(grads['grads'][0][k], 'scatter_' + k)] +
                          [grads['grads'][l][k] for l in range(1, depth)], axis=1) for k in FAMILIES}
    for k in FAMILIES:
        update(k, parts[k])
    for k in gates:
        update(k, all_to_all(grads['gates'][k], 'scatter_' + k))
    update('w_ada', grads['ada'][None])

    sizes = [int(np.prod(given[k].shape)) for k in REPLICATED]
    total = sum(sizes)
    wide = 8 * LANE
    padded = -(-total // (8 * wide)) * (8 * wide)

    def flat(prefix, src):
        v = jnp.concatenate([src[prefix + k].reshape(-1) for k in REPLICATED] + [jnp.zeros((padded - total,), F32)])
        return v.reshape(padded // wide, wide)

    g_all = all_gather(flat('', grads['rep']), 'gather_small_grads')
    res = adamw_call(g_all, flat('', given), flat('m_', given), flat('v_', given), 'adamw_small')
    offs = np.cumsum([0] + sizes)
    for j, k in enumerate(REPLICATED):
        out[k] = [z.reshape(-1)[offs[j]:offs[j + 1]].reshape(given[k].shape) for z in res]

    grad_x = grads['x'][None]
    return (loss, grad_x) + tuple(out[k][j] for j in range(4) for k in WEIGHTS)


def kernel(x, c, ctx, c_ctx, w_ada, b_ada, w_in, w_gla_gate, b_gla_gate, gla_norm_w, s5_lam_re, s5_lam_im, s5_log_dt, s5_b_re, s5_b_im, s5_c_re, s5_c_im, s5_d, w_s5_glu, q_norm_w, k_norm_w, w_proj_gla, w_proj_s5, w_proj_attn, w_out, ln1_w, ln1_b, ln2_w, ln2_b, w_ffn_in, w_ffn_out, loss_target, m_c_ctx, m_w_ada, m_b_ada, m_w_in, m_w_gla_gate, m_b_gla_gate, m_gla_norm_w, m_s5_lam_re, m_s5_lam_im, m_s5_log_dt, m_s5_b_re, m_s5_b_im, m_s5_c_re, m_s5_c_im, m_s5_d, m_w_s5_glu, m_q_norm_w, m_k_norm_w, m_w_proj_gla, m_w_proj_s5, m_w_proj_attn, m_w_out, m_ln1_w, m_ln1_b, m_ln2_w, m_ln2_b, m_w_ffn_in, m_w_ffn_out, v_c_ctx, v_w_ada, v_b_ada, v_w_in, v_w_gla_gate, v_b_gla_gate, v_gla_norm_w, v_s5_lam_re, v_s5_lam_im, v_s5_log_dt, v_s5_b_re, v_s5_b_im, v_s5_c_re, v_s5_c_im, v_s5_d, v_w_s5_glu, v_q_norm_w, v_k_norm_w, v_w_proj_gla, v_w_proj_s5, v_w_proj_attn, v_w_out, v_ln1_w, v_ln1_b, v_ln2_w, v_ln2_b, v_w_ffn_in, v_w_ffn_out):
    return _step(dict(locals()))
```

```python
import functools
import math

import numpy as np
import jax
import jax.numpy as jnp
from jax import lax
from jax.experimental import pallas as pl
from jax.experimental.pallas import tpu as pltpu

F32 = jnp.float32
BF16 = jnp.bfloat16
MESH = pl.DeviceIdType.MESH
N_DEV = 8
VMEM_LIMIT_V7X = 48 * 1024 * 1024
LANE = 128
SINGLE_STEP_K = 2304

GRID_W = 64
GLA_HEADS, GLA_DK, GLA_DV = 4, 128, 256
GLA_QK, GLA_V = GLA_HEADS * GLA_DK, GLA_HEADS * GLA_DV
GLA_RANK, GLA_TAU, GLA_CHUNK = 16, 16.0, 64
S5_WIDTH, S5_GROUP, S5_GROUPS, S5_STATE = 768, 16, 48, 64
S5_CH = S5_GROUPS * S5_STATE
S5_ROWS = S5_CH // LANE
ATT_QH, ATT_KVH, ATT_D = 8, 2, 128
ATT_Q, ATT_KV = ATT_QH * ATT_D, ATT_KVH * ATT_D
ROPE_THETA = 10000.0
EPS = 1e-6
ADAM_LR, ADAM_B1, ADAM_B2, ADAM_EPS, ADAM_WD, ADAM_STEP = 0.001, 0.9, 0.999, 1e-08, 0.01, 10

WEIGHTS = ['c_ctx', 'w_ada', 'b_ada', 'w_in', 'w_gla_gate', 'b_gla_gate', 'gla_norm_w', 's5_lam_re',
           's5_lam_im', 's5_log_dt', 's5_b_re', 's5_b_im', 's5_c_re', 's5_c_im', 's5_d', 'w_s5_glu',
           'q_norm_w', 'k_norm_w', 'w_proj_gla', 'w_proj_s5', 'w_proj_attn', 'w_out', 'ln1_w', 'ln1_b',
           'ln2_w', 'ln2_b', 'w_ffn_in', 'w_ffn_out']
GATHERED = ['w_in', 'w_gla_gate', 'b_gla_gate', 'w_s5_glu', 'w_proj_gla', 'w_proj_s5', 'w_proj_attn',
            'w_out', 'w_ffn_in', 'w_ffn_out']
REPLICATED = [n for n in WEIGHTS if n not in GATHERED and n != 'w_ada']
FAMILIES = ['w_in', 'w_ffn_in_top', 'w_ffn_in_bottom', 'w_ffn_out', 'w_out', 'w_proj_gla', 'w_proj_s5',
            'w_proj_attn', 'w_s5_glu']


def _params(sem=None):
    return pltpu.CompilerParams(dimension_semantics=sem, vmem_limit_bytes=VMEM_LIMIT_V7X)


def _pick(dim, target, align):
    best = None
    t = align
    while t <= min(dim, target):
        if dim % t == 0:
            best = t
        t += align
    return dim if best is None else best


_CONTRACT = {'nn': ((1,), (0,)), 'nt': ((1,), (1,)), 'tn': ((0,), (0,))}


def _dot(a, b, dims):
    return lax.dot_general(a.astype(BF16), b.astype(BF16), (_CONTRACT[dims], ((), ())),
                           preferred_element_type=F32)


@functools.partial(jax.custom_vjp, nondiff_argnums=(2,))
def _mm(a, b, dims):
    return _dot(a, b, dims)


def _mm_fwd(a, b, dims):
    return _dot(a, b, dims), (a, b)


def _mm_bwd(dims, res, g):
    a, b = res
    if dims == 'nn':
        return _dot(g, b, 'nt'), _dot(a, g, 'tn')
    if dims == 'nt':
        return _dot(g, b, 'nn'), _dot(g, a, 'tn')
    return _dot(b, g, 'nt'), _dot(a, g, 'nn')


_mm.defvjp(_mm_fwd, _mm_bwd)


@jax.custom_vjp
def _tri_dot(tri, tri_t, z):
    return lax.dot_general(tri, z, (((1,), (0,)), ((), ())), precision=lax.Precision.HIGHEST,
                           preferred_element_type=F32)


def _tri_dot_fwd(tri, tri_t, z):
    return _tri_dot(tri, tri_t, z), (tri, tri_t)


def _tri_dot_bwd(res, g):
    tri, tri_t = res
    dz = lax.dot_general(tri_t, g, (((1,), (0,)), ((), ())), precision=lax.Precision.HIGHEST,
                         preferred_element_type=F32)
    return jnp.zeros_like(tri), jnp.zeros_like(tri_t), dz


_tri_dot.defvjp(_tri_dot_fwd, _tri_dot_bwd)


def _tri_dot_plain(tri, tri_t, z):
    return lax.dot_general(tri, z, (((1,), (0,)), ((), ())), precision=lax.Precision.HIGHEST,
                           preferred_element_type=F32)


@jax.custom_vjp
def _swap_halves(x):
    return pltpu.roll(x, LANE // 2, 1)


def _swap_fwd(x):
    return pltpu.roll(x, LANE // 2, 1), None


def _swap_bwd(_, g):
    return (pltpu.roll(g, LANE // 2, 1),)


_swap_halves.defvjp(_swap_fwd, _swap_bwd)


def _swap_plain(x):
    return pltpu.roll(x, LANE // 2, 1)


def _coords():
    return lax.axis_index("x"), lax.axis_index("y"), lax.axis_index("c")


def _exchange_out_shape(kind, x):
    if kind == 'gather':
        return (N_DEV,) + x.shape
    return (N_DEV // 2,) + x.shape[1:] if kind == 'pair' else x.shape


def _exchange_copies(kind, x_ref, out_ref, send_sems, recv_sems, local_sem):
    x, y, c = _coords()
    chip, me = 2 * x + y, 4 * x + 2 * y + c
    sibling = (x, y, 1 - c)
    others = [(1 - x, y), (x, 1 - y), (1 - x, 1 - y)]

    def remote(k, src, dst, to):
        return pltpu.make_async_remote_copy(src_ref=src, dst_ref=dst, send_sem=send_sems.at[k],
                                            recv_sem=recv_sems.at[k], device_id=to, device_id_type=MESH)

    if kind == 'gather':
        slot = lambda px, py, pc: out_ref.at[4 * px + 2 * py + pc]
        local = pltpu.make_async_copy(x_ref, out_ref.at[me], local_sem)
        first = [remote(0, x_ref, out_ref.at[me], sibling)]
        first += [remote(1 + j, x_ref, out_ref.at[me], (px, py, c)) for j, (px, py) in enumerate(others)]
        landed = [(remote(1 + j, x_ref, slot(px, py, c), sibling),
                   remote(4 + j, slot(px, py, c), slot(px, py, c), sibling)) for j, (px, py) in enumerate(others)]
        last = [remote(0, x_ref, slot(x, y, 1 - c), sibling)]
        last += [remote(4 + j, x_ref, slot(px, py, 1 - c), sibling) for j, (px, py) in enumerate(others)]
        return local, first, landed, last
    if kind == 'scatter':
        local = pltpu.make_async_copy(x_ref.at[me], out_ref.at[me], local_sem)
        first = []
        for k in range(1, N_DEV):
            px, py, pc = (1 - x if k & 4 else x), (1 - y if k & 2 else y), (1 - c if k & 1 else c)
            first.append(remote(k - 1, x_ref.at[4 * px + 2 * py + pc], out_ref.at[me], (px, py, pc)))
        return local, first, [], first
    if kind == 'pair':
        first = [remote(q, x_ref.at[2 * q + 1 - c], out_ref.at[q], sibling) for q in range(N_DEV // 2)]
        return None, first, [], first
    assert kind == 'quad', kind
    local = pltpu.make_async_copy(x_ref.at[chip], out_ref.at[chip], local_sem)
    first = [remote(j, x_ref.at[2 * px + py], out_ref.at[chip], (px, py, c)) for j, (px, py) in enumerate(others)]
    return local, first, [], first


def _exchange_start(*refs):
    local, first, _, _ = _exchange_copies(*refs)
    if local is not None:
        local.start()
    for cp in first:
        cp.start()


def _exchange_wait(*refs):
    local, first, landed, last = _exchange_copies(*refs)
    for arrival, onward in landed:
        arrival.wait_recv()
        onward.start()
    for cp in last:
        cp.wait_recv()
    for cp in first + [onward for _, onward in landed]:
        cp.wait_send()
    if local is not None:
        local.wait()


_EXCHANGE_SEMS = [pltpu.SemaphoreType.DMA((N_DEV - 1,)), pltpu.SemaphoreType.DMA((N_DEV - 1,)),
                  pltpu.SemaphoreType.DMA]


def pmatmul(a, b, dims, name, exchange=None):
    if dims == 'nn':
        (m, k), (k2, n) = a.shape, b.shape
    elif dims == 'nt':
        (m, k), (n, k2) = a.shape, b.shape
    else:
        (k, m), (k2, n) = a.shape, b.shape
    assert k == k2, (a.shape, b.shape, dims)
    single = k <= SINGLE_STEP_K
    tm = _pick(m, 512 if (single and dims == 'tn') else 768, LANE if dims == 'tn' else 8)
    tn = _pick(n, 1024, LANE)
    tk = k if single else _pick(k, 1024, LANE)
    nk = k // tk
    a_spec = pl.BlockSpec((tk, tm), lambda i, j, kk: (kk, i)) if dims == 'tn' else \
        pl.BlockSpec((tm, tk), lambda i, j, kk: (i, kk))
    b_spec = pl.BlockSpec((tn, tk), lambda i, j, kk: (j, kk)) if dims == 'nt' else \
        pl.BlockSpec((tk, tn), lambda i, j, kk: (kk, j))
    ni, nj = m // tm, n // tn
    acc = [] if single else [pltpu.VMEM((tm, tn), F32)]

    def matmul_step(a_ref, b_ref, o_ref, *acc_ref):
        if single:
            o_ref[...] = _dot(a_ref[...], b_ref[...], dims)
            return
        acc_ref, = acc_ref
        kk = pl.program_id(2)

        @pl.when(kk == 0)
        def _():
            acc_ref[...] = jnp.zeros_like(acc_ref)

        acc_ref[...] += _dot(a_ref[...], b_ref[...], dims)

        @pl.when(kk == nk - 1)
        def _():
            o_ref[...] = acc_ref[...]

    out_spec = pl.BlockSpec((tm, tn), lambda i, j, kk: (i, j))
    if exchange is None:
        return pl.pallas_call(
            matmul_step, name=name, grid=(ni, nj, nk),
            in_specs=[a_spec, b_spec], out_specs=out_spec, out_shape=jax.ShapeDtypeStruct((m, n), F32),
            scratch_shapes=acc, compiler_params=_params(("parallel", "parallel", "arbitrary")),
        )(a, b)

    kind, x = exchange

    def body(a_ref, b_ref, x_ref, o_ref, got_ref, *scratch):
        i, j, kk = pl.program_id(0), pl.program_id(1), pl.program_id(2)
        refs = (kind, x_ref, got_ref) + scratch[len(acc):]

        @pl.when(jnp.logical_and(jnp.logical_and(i == 0, j == 0), kk == 0))
        def _():
            _exchange_start(*refs)

        matmul_step(a_ref, b_ref, o_ref, *scratch[:len(acc)])

        @pl.when(jnp.logical_and(jnp.logical_and(i == ni - 1, j == nj - 1), kk == nk - 1))
        def _():
            _exchange_wait(*refs)

    any_spec = pl.BlockSpec(memory_space=pl.ANY)
    return pl.pallas_call(
        body, name=name, grid=(ni, nj, nk),
        in_specs=[a_spec, b_spec, any_spec], out_specs=[out_spec, any_spec],
        out_shape=[jax.ShapeDtypeStruct((m, n), F32),
                   jax.ShapeDtypeStruct(_exchange_out_shape(kind, x), x.dtype)],
        scratch_shapes=acc + _EXCHANGE_SEMS,
        compiler_params=_params(("arbitrary", "arbitrary", "arbitrary")),
    )(a, b, x)


def make_linear(name):
    @jax.custom_vjp
    def linear(a, w, w_proxy):
        return pmatmul(a, w, 'nn', name + '_fwd')

    def fwd(a, w, w_proxy):
        return pmatmul(a, w, 'nn', name + '_fwd'), (a, w)

    def bwd(res, g):
        a, w = res
        return pmatmul(g, w, 'nt', name + '_da'), jnp.zeros_like(w), pmatmul(a, g, 'tn', name + '_dw')

    linear.defvjp(fwd, bwd)
    return linear


def make_carrier_linear(name):
    def run(a, w, block, parts):
        out, gathered = pmatmul(a, w, 'nn', name + '_fwd', exchange=('gather', block))
        return out, gathered, jnp.zeros((N_DEV,) + parts.shape[1:], parts.dtype)

    @jax.custom_vjp
    def carrier(a, w, w_proxy, block, parts):
        return run(a, w, block, parts)

    def fwd(a, w, w_proxy, block, parts):
        return run(a, w, block, parts), (a, w, block)

    def bwd(res, cts):
        a, w, block = res
        g, _, ct = cts
        da, other_core = pmatmul(g, w, 'nt', name + '_da', exchange=('pair', ct))
        c = lax.axis_index("c")
        mine = lax.dynamic_index_in_dim(ct.reshape((N_DEV // 2, 2) + ct.shape[1:]), c, axis=1, keepdims=False)
        dw, parts = pmatmul(a, g, 'tn', name + '_dw',
                            exchange=('quad', add_halves_call(mine, other_core, name + '_add')))
        return da, jnp.zeros_like(w), dw, jnp.zeros_like(block), parts

    carrier.defvjp(fwd, bwd)
    return carrier


def make_rowwise(name, f, out_widths, tile, n_nondiff=0, f_diff=None):
    f_diff = f if f_diff is None else f_diff

    def fwd_call(rows, params):
        n = rows[0].shape[0]
        nr, npar = len(rows), len(params)

        def body(*refs):
            i = pl.program_id(0)
            outs = f(i, tuple(r[...] for r in refs[:nr]), tuple(p[...] for p in refs[nr:nr + npar]))
            for o_ref, val in zip(refs[nr + npar:], outs):
                o_ref[...] = val

        return pl.pallas_call(
            body, name=name + '_fwd', grid=(n // tile,),
            in_specs=[pl.BlockSpec((tile, r.shape[1]), lambda i: (i, 0)) for r in rows] +
                     [pl.BlockSpec(p.shape, lambda i: (0, 0)) for p in params],
            out_specs=[pl.BlockSpec((tile, w), lambda i: (i, 0)) for w in out_widths],
            out_shape=[jax.ShapeDtypeStruct((n, w), F32) for w in out_widths],
            compiler_params=_params(("parallel",)),
        )(*rows, *params)

    def bwd_call(rows, params, cots):
        n = rows[0].shape[0]
        nr, npar, nc = len(rows), len(params), len(cots)
        nd = nr - n_nondiff

        def body(*refs):
            i = pl.program_id(0)
            rv = tuple(r[...] for r in refs[:nr])
            pv = tuple(p[...] for p in refs[nr:nr + npar])
            cv = tuple(c[...] for c in refs[nr + npar:nr + npar + nc])
            out_refs = refs[nr + npar + nc:]
            _, vjp = jax.vjp(lambda dr, pp: tuple(f_diff(i, dr + rv[nd:], pp)), rv[:nd], pv)
            drows, dparams = vjp(cv)
            for o_ref, val in zip(out_refs[:nd], drows):
                o_ref[...] = val

            @pl.when(i == 0)
            def _():
                for o_ref in out_refs[nd:]:
                    o_ref[...] = jnp.zeros_like(o_ref)

            for o_ref, val in zip(out_refs[nd:], dparams):
                o_ref[...] += val

        outs = pl.pallas_call(
            body, name=name + '_bwd', grid=(n // tile,),
            in_specs=[pl.BlockSpec((tile, r.shape[1]), lambda i: (i, 0)) for r in rows] +
                     [pl.BlockSpec(p.shape, lambda i: (0, 0)) for p in params] +
                     [pl.BlockSpec((tile, c.shape[1]), lambda i: (i, 0)) for c in cots],
            out_specs=[pl.BlockSpec((tile, r.shape[1]), lambda i: (i, 0)) for r in rows[:nd]] +
                      [pl.BlockSpec(p.shape, lambda i: (0, 0)) for p in params],
            out_shape=[jax.ShapeDtypeStruct(r.shape, F32) for r in rows[:nd]] +
                      [jax.ShapeDtypeStruct(p.shape, F32) for p in params],
            compiler_params=_params(("arbitrary",)),
        )(*rows, *params, *cots)
        drows = tuple(outs[:nd]) + tuple(jnp.zeros_like(r) for r in rows[nd:])
        return drows, tuple(outs[nd:])

    @jax.custom_vjp
    def op(rows, params):
        return tuple(fwd_call(rows, params))

    def fwd(rows, params):
        return tuple(fwd_call(rows, params)), (rows, params)

    def bwd(res, cots):
        rows, params = res
        return bwd_call(rows, params, tuple(cots))

    op.defvjp(fwd, bwd)
    return op


def _silu(x):
    return x * jax.nn.sigmoid(x)


def _layer_norm(z, w, b):
    mu = jnp.mean(z, -1, keepdims=True)
    zc = z - mu
    var = jnp.mean(zc * zc, -1, keepdims=True)
    y = zc * lax.rsqrt(var + EPS)
    return y if w is None else y * w + b


def _softmax_rows(s):
    e = jnp.exp(s - jnp.max(s, -1, keepdims=True))
    return e / jnp.sum(e, -1, keepdims=True)


def attention_fwd_call(q, k, v, n_ctx):
    n = q.shape[0]
    tq = n_ctx
    grp = ATT_QH // ATT_KVH
    scale = ATT_D ** -0.5

    def body(q_ref, k_ref, v_ref, o_ref):
        def run(nk):
            p = _softmax_rows(_dot(q_ref[...], k_ref[0:nk, :], 'nt') * scale)
            o_ref[...] = _dot(p, v_ref[0:nk, :], 'nn')

        pl.when(pl.program_id(2) == 0)(lambda: run(n_ctx))
        pl.when(pl.program_id(2) > 0)(lambda: run(n))

    return pl.pallas_call(
        body, name='attn_fwd', grid=(ATT_KVH, grp, n // tq),
        in_specs=[pl.BlockSpec((tq, ATT_D), lambda h, g, i: (i, h * grp + g)),
                  pl.BlockSpec((n, ATT_D), lambda h, g, i: (0, h)),
                  pl.BlockSpec((n, ATT_D), lambda h, g, i: (0, h))],
        out_specs=pl.BlockSpec((tq, ATT_D), lambda h, g, i: (i, h * grp + g)),
        out_shape=jax.ShapeDtypeStruct((n, ATT_Q), F32),
        compiler_params=_params(("parallel", "parallel", "parallel")),
    )(q, k, v)


def attention_bwd_call(q, k, v, do, n_ctx):
    n = q.shape[0]
    tq = n_ctx
    grp = ATT_QH // ATT_KVH
    scale = ATT_D ** -0.5

    def body(q_ref, k_ref, v_ref, do_ref, dq_ref, dk_ref, dv_ref):
        @pl.when(jnp.logical_and(pl.program_id(1) == 0, pl.program_id(2) == 0))
        def _():
            dk_ref[...] = jnp.zeros_like(dk_ref)
            dv_ref[...] = jnp.zeros_like(dv_ref)

        def run(nk):
            qv, kv, vv, dov = q_ref[...], k_ref[0:nk, :], v_ref[0:nk, :], do_ref[...]
            p = _softmax_rows(_dot(qv, kv, 'nt') * scale)
            dv_ref[0:nk, :] += _dot(p, dov, 'tn')
            dp = _dot(dov, vv, 'nt')
            ds = p * (dp - jnp.sum(dp * p, -1, keepdims=True)) * scale
            dq_ref[...] = _dot(ds, kv, 'nn')
            dk_ref[0:nk, :] += _dot(ds, qv, 'tn')

        pl.when(pl.program_id(2) == 0)(lambda: run(n_ctx))
        pl.when(pl.program_id(2) > 0)(lambda: run(n))

    return pl.pallas_call(
        body, name='attn_bwd', grid=(ATT_KVH, grp, n // tq),
        in_specs=[pl.BlockSpec((tq, ATT_D), lambda h, g, i: (i, h * grp + g)),
                  pl.BlockSpec((n, ATT_D), lambda h, g, i: (0, h)),
                  pl.BlockSpec((n, ATT_D), lambda h, g, i: (0, h)),
                  pl.BlockSpec((tq, ATT_D), lambda h, g, i: (i, h * grp + g))],
        out_specs=[pl.BlockSpec((tq, ATT_D), lambda h, g, i: (i, h * grp + g)),
                   pl.BlockSpec((n, ATT_D), lambda h, g, i: (0, h)),
                   pl.BlockSpec((n, ATT_D), lambda h, g, i: (0, h))],
        out_shape=[jax.ShapeDtypeStruct((n, ATT_Q), F32), jax.ShapeDtypeStruct((n, ATT_KV), F32),
                   jax.ShapeDtypeStruct((n, ATT_KV), F32)],
        compiler_params=_params(("parallel", "arbitrary", "arbitrary")),
    )(q, k, v, do)


def make_attention(n_ctx):
    @jax.custom_vjp
    def attention(q, k, v):
        return attention_fwd_call(q, k, v, n_ctx)

    def fwd(q, k, v):
        return attention_fwd_call(q, k, v, n_ctx), (q, k, v)

    def bwd(res, do):
        return tuple(attention_bwd_call(*res, do, n_ctx))

    attention.defvjp(fwd, bwd)
    return attention


def _log_sigmoid(z):
    return jnp.minimum(z, 0.0) - jnp.log(1.0 + jnp.exp(-jnp.abs(z)))


def _gla_chunk(d, q, k, v, glr, wg, bg, state, mm, tri_dot):
    chunk = q.shape[0]
    r = lax.broadcasted_iota(jnp.int32, (chunk, chunk), 0)
    c = lax.broadcasted_iota(jnp.int32, (chunk, chunk), 1)
    lower = jnp.where(c <= r, 1.0, 0.0).astype(F32)
    upper = jnp.where(c >= r, 1.0, 0.0).astype(F32)
    tri = jnp.where(d == 0, lower, upper)
    tri_t = jnp.where(d == 0, upper, lower)
    log_a = _log_sigmoid(mm(glr, wg, 'nn') + bg) * (1.0 / GLA_TAU)
    outs, new_state = [], []
    for h in range(GLA_HEADS):
        la = log_a[:, h * GLA_DK:(h + 1) * GLA_DK]
        b = tri_dot(tri, tri_t, la)
        tot = jnp.sum(la, axis=0, keepdims=True)
        qh = q[:, h * GLA_DK:(h + 1) * GLA_DK] * (GLA_DK ** -0.5)
        kh = k[:, h * GLA_DK:(h + 1) * GLA_DK]
        vh = v[:, h * GLA_DV:(h + 1) * GLA_DV]
        qe = qh * jnp.exp(b)
        ke = kh * jnp.exp(-b)
        att = mm(qe, ke, 'nt') * tri
        outs.append(mm(att, vh, 'nn') + mm(qe, state[h], 'nn'))
        kd = kh * jnp.exp(tot - b)
        decay = jnp.transpose(jnp.broadcast_to(jnp.exp(tot), (GLA_DK, GLA_DK)))
        decay = jnp.concatenate([decay] * (GLA_DV // GLA_DK), axis=1)
        new_state.append(decay * state[h] + mm(kd, vh, 'tn'))
    return jnp.concatenate(outs, axis=1), tuple(new_state)


def _gla_chunk_of(d, s, nc, ns):
    falling = jnp.where(s < nc, nc - 1 - s, ns - 1 - (s - nc))
    return jnp.where(d == 0, s, falling)


def gla_fwd_call(q, k, v, glr, wg, bg, n_ctx):
    n = q.shape[0]
    ch = GLA_CHUNK
    ns, nc = n // ch, n_ctx // ch

    def at(d, s):
        return _gla_chunk_of(d, s, nc, ns)

    def body(q_ref, k_ref, v_ref, glr_ref, wg_ref, bg_ref, o_ref, saved_ref, state_ref):
        d, s = pl.program_id(0), pl.program_id(1)

        @pl.when(s == 0)
        def _():
            state_ref[...] = jnp.zeros_like(state_ref)

        saved_ref[...] = state_ref[...]
        state = tuple(state_ref[h] for h in range(GLA_HEADS))
        o, new_state = _gla_chunk(d, q_ref[...], k_ref[...], v_ref[...], glr_ref[...], wg_ref[...],
                                  bg_ref[...], state, _dot, _tri_dot_plain)
        o_ref[...] = o
        for h in range(GLA_HEADS):
            state_ref[h] = new_state[h]

    return pl.pallas_call(
        body, name='gla_fwd', grid=(2, ns),
        in_specs=[pl.BlockSpec((ch, GLA_QK), lambda d, s: (at(d, s), 0)),
                  pl.BlockSpec((ch, GLA_QK), lambda d, s: (at(d, s), 0)),
                  pl.BlockSpec((ch, GLA_V), lambda d, s: (at(d, s), 0)),
                  pl.BlockSpec((ch, LANE), lambda d, s: (at(d, s), 0)),
                  pl.BlockSpec((None, LANE, GLA_QK), lambda d, s: (d, 0, 0)),
                  pl.BlockSpec((None, 1, GLA_QK), lambda d, s: (d, 0, 0))],
        out_specs=[pl.BlockSpec((None, ch, GLA_V), lambda d, s: (d, at(d, s), 0)),
                   pl.BlockSpec((None, None, GLA_HEADS, GLA_DK, GLA_DV), lambda d, s: (d, s, 0, 0, 0))],
        out_shape=[jax.ShapeDtypeStruct((2, n, GLA_V), F32),
                   jax.ShapeDtypeStruct((2, ns, GLA_HEADS, GLA_DK, GLA_DV), F32)],
        scratch_shapes=[pltpu.VMEM((GLA_HEADS, GLA_DK, GLA_DV), F32)],
        compiler_params=_params(("parallel", "arbitrary")),
    )(q, k, v, glr, wg, bg)


def gla_bwd_call(q, k, v, glr, wg, bg, saved, do, n_ctx):
    n = q.shape[0]
    ch = GLA_CHUNK
    ns, nc = n // ch, n_ctx // ch

    def at(d, s):
        return _gla_chunk_of(d, ns - 1 - s, nc, ns)

    def body(q_ref, k_ref, v_ref, glr_ref, wg_ref, bg_ref, saved_ref, do_ref,
             dq_ref, dk_ref, dv_ref, dglr_ref, dwg_ref, dbg_ref, dstate_ref):
        d, s = pl.program_id(0), pl.program_id(1)

        @pl.when(s == 0)
        def _():
            dstate_ref[...] = jnp.zeros_like(dstate_ref)
            dwg_ref[...] = jnp.zeros_like(dwg_ref)
            dbg_ref[...] = jnp.zeros_like(dbg_ref)

        state = tuple(saved_ref[h] for h in range(GLA_HEADS))
        dstate = tuple(dstate_ref[h] for h in range(GLA_HEADS))

        def f(qv, kv, vv, glrv, wgv, bgv, st):
            return _gla_chunk(d, qv, kv, vv, glrv, wgv, bgv, st, _mm, _tri_dot)

        _, vjp = jax.vjp(f, q_ref[...], k_ref[...], v_ref[...], glr_ref[...], wg_ref[...], bg_ref[...], state)
        dq, dk, dv, dglr, dwg, dbg, dprev = vjp((do_ref[...], dstate))
        dq_ref[...] = dq
        dk_ref[...] = dk
        dv_ref[...] = dv
        dglr_ref[...] = dglr
        dwg_ref[...] += dwg
        dbg_ref[...] += dbg
        for h in range(GLA_HEADS):
            dstate_ref[h] = dprev[h]

    return pl.pallas_call(
        body, name='gla_bwd', grid=(2, ns),
        in_specs=[pl.BlockSpec((ch, GLA_QK), lambda d, s: (at(d, s), 0)),
                  pl.BlockSpec((ch, GLA_QK), lambda d, s: (at(d, s), 0)),
                  pl.BlockSpec((ch, GLA_V), lambda d, s: (at(d, s), 0)),
                  pl.BlockSpec((ch, LANE), lambda d, s: (at(d, s), 0)),
                  pl.BlockSpec((None, LANE, GLA_QK), lambda d, s: (d, 0, 0)),
                  pl.BlockSpec((None, 1, GLA_QK), lambda d, s: (d, 0, 0)),
                  pl.BlockSpec((None, None, GLA_HEADS, GLA_DK, GLA_DV), lambda d, s: (d, ns - 1 - s, 0, 0, 0)),
                  pl.BlockSpec((None, ch, GLA_V), lambda d, s: (d, at(d, s), 0))],
        out_specs=[pl.BlockSpec((None, ch, GLA_QK), lambda d, s: (d, at(d, s), 0)),
                   pl.BlockSpec((None, ch, GLA_QK), lambda d, s: (d, at(d, s), 0)),
                   pl.BlockSpec((None, ch, GLA_V), lambda d, s: (d, at(d, s), 0)),
                   pl.BlockSpec((None, ch, LANE), lambda d, s: (d, at(d, s), 0)),
                   pl.BlockSpec((None, LANE, GLA_QK), lambda d, s: (d, 0, 0)),
                   pl.BlockSpec((None, 1, GLA_QK), lambda d, s: (d, 0, 0))],
        out_shape=[jax.ShapeDtypeStruct((2, n, GLA_QK), F32), jax.ShapeDtypeStruct((2, n, GLA_QK), F32),
                   jax.ShapeDtypeStruct((2, n, GLA_V), F32), jax.ShapeDtypeStruct((2, n, LANE), F32),
                   jax.ShapeDtypeStruct((2, LANE, GLA_QK), F32), jax.ShapeDtypeStruct((2, 1, GLA_QK), F32)],
        scratch_shapes=[pltpu.VMEM((GLA_HEADS, GLA_DK, GLA_DV), F32)],
        compiler_params=_params(("parallel", "arbitrary")),
    )(q, k, v, glr, wg, bg, saved, do)


def make_gla(n_ctx):
    @jax.custom_vjp
    def gla(q, k, v, glr, wg, bg):
        return gla_fwd_call(q, k, v, glr, wg, bg, n_ctx)[0]

    def fwd(q, k, v, glr, wg, bg):
        o, saved = gla_fwd_call(q, k, v, glr, wg, bg, n_ctx)
        return o, (q, k, v, glr, wg, bg, saved)

    def bwd(res, do):
        dq, dk, dv, dglr, dwg, dbg = gla_bwd_call(*res, do, n_ctx)
        return dq[0] + dq[1], dk[0] + dk[1], dv[0] + dv[1], dglr[0] + dglr[1], dwg, dbg

    gla.defvjp(fwd, bwd)
    return gla


def _s5_chunk_at(kind, g, nch):
    if kind == 0:
        return g
    if kind == 1:
        return jnp.where(g == 0, 0, nch - g)
    if kind == 2:
        return nch - 1 - g
    return jnp.where(g == nch - 1, 0, g + 1)


def s5_scan_call(xr, xi, ar, ai, kind, lc, name):
    nch = xr.shape[0] // lc
    rising = kind in (0, 3)

    def body(xr_ref, xi_ref, ar_ref, ai_ref, sr_ref, si_ref, st_ref):
        g = pl.program_id(0)

        @pl.when(g == 0)
        def _():
            st_ref[...] = jnp.zeros_like(st_ref)

        a_r, a_i = ar_ref[...], ai_ref[...]

        def step(j, carry):
            cr, ci = carry
            tt = j if rising else lc - 1 - j
            nr = a_r * cr - a_i * ci + xr_ref[tt]
            ni = a_r * ci + a_i * cr + xi_ref[tt]
            sr_ref[tt] = nr
            si_ref[tt] = ni
            return nr, ni

        cr, ci = lax.fori_loop(0, lc, step, (st_ref[0], st_ref[1]), unroll=8)
        st_ref[0] = cr
        st_ref[1] = ci

    blk = pl.BlockSpec((lc, S5_ROWS, LANE), lambda g: (_s5_chunk_at(kind, g, nch), 0, 0))
    par = pl.BlockSpec((S5_ROWS, LANE), lambda g: (0, 0))
    return pl.pallas_call(
        body, name=name, grid=(nch,), in_specs=[blk, blk, par, par], out_specs=[blk, blk],
        scratch_shapes=[pltpu.VMEM((2, S5_ROWS, LANE), F32)],
        out_shape=[jax.ShapeDtypeStruct(xr.shape, F32), jax.ShapeDtypeStruct(xr.shape, F32)],
        compiler_params=_params(("arbitrary",)),
    )(xr, xi, ar, ai)


def s5_da_call(lr, li, pr, pi, lc):
    t = lr.shape[0]

    def body(lr_ref, li_ref, pr_ref, pi_ref, dar_ref, dai_ref):
        @pl.when(pl.program_id(0) == 0)
        def _():
            dar_ref[...] = jnp.zeros_like(dar_ref)
            dai_ref[...] = jnp.zeros_like(dai_ref)

        a, b, c, e = lr_ref[...], li_ref[...], pr_ref[...], pi_ref[...]
        dar_ref[...] += jnp.sum(a * c + b * e, axis=0)
        dai_ref[...] += jnp.sum(b * c - a * e, axis=0)

    blk = pl.BlockSpec((lc, S5_ROWS, LANE), lambda g: (g, 0, 0))
    par = pl.BlockSpec((S5_ROWS, LANE), lambda g: (0, 0))
    return pl.pallas_call(
        body, name='s5_da', grid=(t // lc,), in_specs=[blk] * 4, out_specs=[par, par],
        out_shape=[jax.ShapeDtypeStruct((S5_ROWS, LANE), F32)] * 2,
        compiler_params=_params(("arbitrary",)),
    )(lr, li, pr, pi)


def make_s5_scan(direction, n_chunks, lc):
    def predecessors(s):
        zero = jnp.zeros((1,) + s.shape[1:], F32)
        if direction == 0:
            return jnp.concatenate([zero, s[:-1]])
        return jnp.concatenate([s[1:lc], zero, s[lc + 1:], s[0:1]])

    @jax.custom_vjp
    def scan(xr, xi, ar, ai):
        return tuple(s5_scan_call(xr, xi, ar, ai, direction, lc, 's5_scan_d%d' % direction))

    def fwd(xr, xi, ar, ai):
        sr, si = s5_scan_call(xr, xi, ar, ai, direction, lc, 's5_scan_d%d' % direction)
        return (sr, si), (sr, si, ar, ai)

    def bwd(res, g):
        sr, si, ar, ai = res
        lr, li = s5_scan_call(g[0], g[1], ar, -ai, 2 + direction, lc, 's5_adjoint_d%d' % direction)
        dar, dai = s5_da_call(lr, li, predecessors(sr), predecessors(si), lc)
        return lr, li, dar, dai

    scan.defvjp(fwd, bwd)
    return scan


def loss_fwd_call(y, target, tile):
    n, dm = y.shape

    def body(y_ref, t_ref, o_ref):
        e = y_ref[...] - t_ref[...]
        o_ref[...] = jnp.full(o_ref.shape, 0.5 * jnp.sum(e * e) / dm, F32)

    out = pl.pallas_call(
        body, name='loss_fwd', grid=(n // tile,),
        in_specs=[pl.BlockSpec((tile, dm), lambda i: (i, 0))] * 2,
        out_specs=pl.BlockSpec((8, LANE), lambda i: (i, 0)),
        out_shape=jax.ShapeDtypeStruct((8 * (n // tile), LANE), F32),
        compiler_params=_params(("parallel",)),
    )(y, target)
    return jnp.sum(out[::8, 0])


def make_loss(tile):
    @jax.custom_vjp
    def loss(y, target):
        return loss_fwd_call(y, target, tile)

    def fwd(y, target):
        return loss_fwd_call(y, target, tile), (y, target)

    def bwd(res, g):
        y, target = res
        scale = jnp.full((1, y.shape[1]), g / y.shape[1], F32)

        def body(y_ref, t_ref, s_ref, o_ref):
            o_ref[...] = (y_ref[...] - t_ref[...]) * s_ref[...]

        dy = pl.pallas_call(
            body, name='loss_bwd', grid=(y.shape[0] // tile,),
            in_specs=[pl.BlockSpec((tile, y.shape[1]), lambda i: (i, 0))] * 2 +
                     [pl.BlockSpec((1, y.shape[1]), lambda i: (0, 0))],
            out_specs=pl.BlockSpec((tile, y.shape[1]), lambda i: (i, 0)),
            out_shape=jax.ShapeDtypeStruct(y.shape, F32),
            compiler_params=_params(("parallel",)),
        )(y, target, scale)
        return dy, jnp.zeros_like(target)

    loss.defvjp(fwd, bwd)
    return loss


def all_gather(x, name):
    return exchange_call('gather', x, name)


def all_to_all(x, name):
    return exchange_call('scatter', x, name)


def exchange_call(kind, x, name):
    def body(x_ref, out_ref, send_sems, recv_sems, local_sem):
        refs = (kind, x_ref, out_ref, send_sems, recv_sems, local_sem)
        _exchange_start(*refs)
        _exchange_wait(*refs)

    return pl.pallas_call(
        body, name=name, out_shape=jax.ShapeDtypeStruct(_exchange_out_shape(kind, x), x.dtype),
        in_specs=[pl.BlockSpec(memory_space=pl.ANY)], out_specs=pl.BlockSpec(memory_space=pl.ANY),
        scratch_shapes=_EXCHANGE_SEMS,
    )(x)


def add_halves_call(a, b, name):
    p, r, c = a.shape
    tr = _pick(r, 256, 16)

    def body(a_ref, b_ref, o_ref):
        o_ref[...] = (a_ref[...].astype(F32) + b_ref[...].astype(F32)).astype(o_ref.dtype)

    blk = pl.BlockSpec((None, tr, c), lambda q, i: (q, i, 0))
    return pl.pallas_call(
        body, name=name, grid=(p, r // tr), in_specs=[blk, blk], out_specs=blk,
        out_shape=jax.ShapeDtypeStruct(a.shape, BF16), compiler_params=_params(("parallel", "parallel")),
    )(a, b)


def reduce_scatter_parts(g, name):
    c = lax.axis_index("c")
    mine = lax.dynamic_index_in_dim(g.reshape((N_DEV // 2, 2) + g.shape[1:]), c, axis=1, keepdims=False)
    return exchange_call('quad', add_halves_call(mine, exchange_call('pair', g, name + '_pair'), name + '_add'),
                         name + '_quad')


def sum_parts_call(parts, name):
    p, r, c = parts.shape
    tr = _pick(r, 256, 8)

    def body(p_ref, o_ref):
        acc = p_ref[0].astype(F32)
        for j in range(1, p):
            acc = acc + p_ref[j].astype(F32)
        o_ref[...] = acc

    return pl.pallas_call(
        body, name=name, grid=(r // tr,), in_specs=[pl.BlockSpec((p, tr, c), lambda i: (0, i, 0))],
        out_specs=pl.BlockSpec((tr, c), lambda i: (i, 0)), out_shape=jax.ShapeDtypeStruct((r, c), F32),
        compiler_params=_params(("parallel",)),
    )(parts)


@jax.custom_vjp
def gather_rows(x):
    return all_gather(x, 'mod_gather')


def _gather_rows_fwd(x):
    return all_gather(x, 'mod_gather'), None


def _gather_rows_bwd(_, ct):
    return (sum_parts_call(all_to_all(ct, 'mod_scatter'), 'mod_scatter_sum'),)


gather_rows.defvjp(_gather_rows_fwd, _gather_rows_bwd)


def adamw_call(parts, w, m, v, name):
    p, r, c = parts.shape
    tr = _pick(r, 128, 8)
    c1 = 1.0 / (1.0 - ADAM_B1 ** ADAM_STEP)
    c2 = 1.0 / (1.0 - ADAM_B2 ** ADAM_STEP)

    def body(p_ref, w_ref, m_ref, v_ref, g_ref, d_ref, m2_ref, v2_ref):
        g = p_ref[0].astype(F32)
        for j in range(1, p):
            g = g + p_ref[j].astype(F32)
        m2 = ADAM_B1 * m_ref[...] + (1.0 - ADAM_B1) * g
        v2 = ADAM_B2 * v_ref[...] + (1.0 - ADAM_B2) * (g * g)
        g_ref[...] = g
        m2_ref[...] = m2
        v2_ref[...] = v2
        d_ref[...] = -ADAM_LR * ((m2 * c1) / (jnp.sqrt(v2 * c2) + ADAM_EPS) + ADAM_WD * w_ref[...])

    blk = pl.BlockSpec((tr, c), lambda i: (i, 0))
    return pl.pallas_call(
        body, name=name, grid=(r // tr,),
        in_specs=[pl.BlockSpec((p, tr, c), lambda i: (0, i, 0)), blk, blk, blk], out_specs=[blk] * 4,
        out_shape=[jax.ShapeDtypeStruct((r, c), F32)] * 4,
        compiler_params=_params(("parallel",)),
    )(parts, w, m, v)


def _deinterleave(z, heads):
    lead = z.shape[:-1]
    return z.reshape(lead + (heads, ATT_D // 2, 2)).swapaxes(-1, -2).reshape(lead + (heads * ATT_D,))


def _unshard_cols(g):
    return jnp.moveaxis(g, 0, -2).reshape(g.shape[1:-1] + (N_DEV * g.shape[-1],))


def _pack_w_in(g, dm):
    w = _unshard_cols(g)
    splits = (GLA_QK, GLA_QK, GLA_V, GLA_V, GLA_RANK, S5_WIDTH, ATT_Q, ATT_KV, ATT_KV, 3 * dm)
    gq, gk, gv, gr, glr, su, aq, ak, av, bg = jnp.split(w, np.cumsum(splits)[:-1].tolist(), axis=-1)
    pad = jnp.zeros(w.shape[:-1] + (LANE - GLA_RANK,), w.dtype)
    return jnp.concatenate([gq, gk, gv, gr, su, _deinterleave(aq, ATT_QH), _deinterleave(ak, ATT_KVH),
                            av, bg, glr, pad], axis=-1)


def _in_widths(dm):
    return (GLA_QK, GLA_QK, GLA_V, GLA_V, S5_WIDTH, ATT_Q, ATT_KV, ATT_KV, 3 * dm, LANE)


def _make_split(widths):
    cuts = np.cumsum(widths)[:-1].tolist()

    @jax.custom_vjp
    def split(z):
        return tuple(jnp.split(z, cuts, axis=1))

    def fwd(z):
        return tuple(jnp.split(z, cuts, axis=1)), None

    def bwd(_, cots):
        return (jnp.concatenate(cots, axis=1),)

    split.defvjp(fwd, bwd)
    return split


def _block_diag_in(b):
    eye = jnp.eye(S5_GROUPS, dtype=F32)
    return jnp.einsum('gpc,gh->gchp', b, eye).reshape(S5_WIDTH, S5_CH)


def _block_diag_out(cm):
    eye = jnp.eye(S5_GROUPS, dtype=F32)
    return jnp.einsum('gcp,gh->gphc', cm, eye).reshape(S5_CH, S5_WIDTH)


def _s5_discretise(lam_re, lam_im, log_dt):
    dt = jnp.exp(log_dt)[:, None]
    mag = jnp.exp(lam_re * dt)
    a_re, a_im = mag * jnp.cos(lam_im * dt), mag * jnp.sin(lam_im * dt)
    den = lam_re * lam_re + lam_im * lam_im
    nr, ni = a_re - 1, a_im
    k_re = (nr * lam_re + ni * lam_im) / den
    k_im = (ni * lam_re - nr * lam_im) / den
    return a_re, a_im, k_re, k_im


def _rope_tables(n_ctx, n_lat):
    rows = jnp.repeat(jnp.arange(n_lat // GRID_W), GRID_W).astype(F32)
    cols = jnp.tile(jnp.arange(GRID_W), n_lat // GRID_W).astype(F32)
    n_freq = ATT_D // 4
    inv = ROPE_THETA ** (-jnp.arange(n_freq, dtype=F32) / n_freq)
    ang = jnp.concatenate([rows[:, None] * inv, cols[:, None] * inv], -1)
    cos, sin = jnp.cos(ang), jnp.sin(ang)
    cosf = jnp.concatenate([jnp.ones((n_ctx, ATT_D), F32), jnp.concatenate([cos, cos], -1)], 0)
    sinf = jnp.concatenate([jnp.zeros((n_ctx, ATT_D), F32), jnp.concatenate([-sin, sin], -1)], 0)
    return cosf, sinf


def _build_ops(n, n_ctx, dm, dff, depth):
    t256 = min(256, n_ctx)
    t64 = 64
    alpha = (2 * depth) ** 0.25
    ops = {}

    def seg(i, tile, pc, pl_):
        return jnp.where(i < n_ctx // tile, pc, pl_)

    def modulate_f(i, rows, params):
        sh = seg(i, t256, params[0], params[1])
        sc = seg(i, t256, params[2], params[3])
        return (rows[0] * (1.0 + sc) + sh,)

    def postnorm_f(i, rows, params):
        g = seg(i, t256, params[0], params[1])
        return (_layer_norm(alpha * rows[0] + g * rows[1], params[2], params[3]),)

    def gla_out_f(i, rows, params):
        o = rows[0] + rows[1]
        heads = [_layer_norm(o[:, h * GLA_DV:(h + 1) * GLA_DV], None, None) for h in range(GLA_HEADS)]
        return (jnp.concatenate(heads, axis=1) * params[0] * _silu(rows[2]),)

    def s5_in_f(i, rows, params):
        br, bi = rows[0][:, :S5_CH], rows[0][:, S5_CH:]
        kr0, ki0, kr1, ki1 = params
        return (kr0 * br - ki0 * bi, kr0 * bi + ki0 * br, kr1 * br - ki1 * bi, kr1 * bi + ki1 * br)

    def s5_mid_f(i, rows, params):
        return (jax.nn.gelu(rows[0] + rows[1] * params[0]),)

    def s5_gate_f(i, rows, params):
        return (rows[0] * jax.nn.sigmoid(rows[1]),)

    def qk_f(swap):
        def f(i, rows, params):
            aq, ak, cosf, sinf = rows
            outs = []
            for z, w, heads in ((aq, params[0], ATT_QH), (ak, params[1], ATT_KVH)):
                hs = []
                for h in range(heads):
                    zh = z[:, h * ATT_D:(h + 1) * ATT_D]
                    zn = zh * lax.rsqrt(jnp.mean(zh * zh, -1, keepdims=True) + EPS) * w
                    hs.append(zn * cosf + swap(zn) * sinf)
                outs.append(jnp.concatenate(hs, axis=1))
            return tuple(outs)
        return f

    def silu_f(i, rows, params):
        return (_silu(rows[0]),)

    def merge_f(i, rows, params):
        pg, ps, pa, bg = rows
        gate = jax.nn.sigmoid(bg)
        return (gate[:, :dm] * pg + gate[:, dm:2 * dm] * ps + gate[:, 2 * dm:] * pa,)

    def swiglu_f(i, rows, params):
        return (_silu(rows[0]) * rows[1],)

    ops['modulate1'] = make_rowwise('modulate1', modulate_f, (dm,), t256)
    ops['modulate2'] = make_rowwise('modulate2', modulate_f, (dm,), t256)
    ops['postnorm1'] = make_rowwise('postnorm1', postnorm_f, (dm,), t256)
    ops['postnorm2'] = make_rowwise('postnorm2', postnorm_f, (dm,), t256)
    ops['gla_out'] = make_rowwise('gla_out', gla_out_f, (GLA_V,), t256)
    ops['s5_in'] = make_rowwise('s5_in', s5_in_f, (S5_CH,) * 4, t64)
    ops['s5_mid'] = make_rowwise('s5_mid', s5_mid_f, (S5_WIDTH,), t256)
    ops['s5_gate'] = make_rowwise('s5_gate', s5_gate_f, (S5_WIDTH,), t256)
    ops['merge'] = make_rowwise('merge', merge_f, (dm,), t64)
    ops['swiglu'] = make_rowwise('swiglu', swiglu_f, (dff,), t64)
    ops['qk_prep'] = make_rowwise('qk_prep', qk_f(_swap_plain), (ATT_Q, ATT_KV), t256, n_nondiff=2,
                                  f_diff=qk_f(_swap_halves))
    ops['silu_c'] = make_rowwise('silu_c', silu_f, (dm,), 16)
    return ops


def _flat2d(a):
    return a.reshape(-1, a.shape[-1])


def _step(given):
    x, c, ctx = given['x'][0], given['c'], given['ctx'][0]
    target = given['loss_target'][0]
    n_lat, dm = x.shape
    n_ctx = ctx.shape[0]
    n = n_ctx + n_lat
    depth = given['w_ada'].shape[0]
    dff = given['w_ffn_out'].shape[1] * N_DEV
    ada_cols = given['w_ada'].shape[2]
    xi, yi, ci = _coords()
    me = 4 * xi + 2 * yi + ci
    lc = n_ctx
    ops = _build_ops(n, n_ctx, dm, dff, depth)
    names = ['ada', 'in', 's5b', 's5c0', 's5c1', 's5c2', 's5c3', 'glu', 'pg', 'ps', 'pa', 'out', 'fa', 'fb', 'fo']
    lin = {k: make_linear('lin_' + k) for k in names}
    split_in = _make_split(_in_widths(dm))
    gla = make_gla(n_ctx)
    attention = make_attention(n_ctx)
    scans = [make_s5_scan(d, n // lc, lc) for d in range(2)]
    loss_op = make_loss(min(256, n_lat))
    cosf, sinf = _rope_tables(n_ctx, n_lat)

    carried_by = {'w_in': 'in', 'w_ffn_in_top': 'fa', 'w_ffn_in_bottom': 'fb', 'w_ffn_out': 'fo', 'w_out': 'out',
                  'w_proj_gla': 'pg', 'w_proj_s5': 'ps', 'w_proj_attn': 'pa', 'w_s5_glu': 'glu'}
    carrier = {k: make_carrier_linear('lin_' + k) for k in carried_by.values()}
    half = dm // 2

    def shard(k, l):
        if k == 'w_ffn_in_top':
            return given['w_ffn_in'][l, :half]
        return given['w_ffn_in'][l, half:] if k == 'w_ffn_in_bottom' else given[k][l]

    blocks = [{k: shard(k, l).astype(BF16) for k in FAMILIES} for l in range(depth)]
    gathered0 = {k: all_gather(blocks[0][k], 'gather_' + k) for k in FAMILIES}
    gates = {k: all_gather(given[k], 'gather_' + k) for k in ('w_gla_gate', 'b_gla_gate')}
    c_all = all_gather(c, 'gather_c').reshape(N_DEV, dm)

    diff0 = {
        'x': x,
        'rep': {k: given[k] for k in REPLICATED},
        'grads': [{k: jnp.zeros((N_DEV if l == 0 else N_DEV // 2,) + blocks[l][k].shape, BF16) for k in FAMILIES}
                  for l in range(depth)],
        'gates': {k: jnp.zeros((N_DEV,) + given[k].shape, F32) for k in gates},
        'ada': jnp.zeros(given['w_ada'].shape, F32),
    }
    w_ada_bf = given['w_ada'].astype(BF16)

    def unpack(g):
        w_fi = jnp.concatenate([_unshard_cols(g['w_ffn_in_top']), _unshard_cols(g['w_ffn_in_bottom'])], axis=0)
        rows = lambda z: z.reshape((N_DEV * z.shape[1],) + z.shape[2:])
        return dict(w_in=_pack_w_in(g['w_in'], dm), w_fa=w_fi[:, :dff], w_fb=w_fi[:, dff:],
                    w_fo=rows(g['w_ffn_out']), w_out=rows(g['w_out']), w_pg=_unshard_cols(g['w_proj_gla']),
                    w_ps=_unshard_cols(g['w_proj_s5']), w_pa=_unshard_cols(g['w_proj_attn']),
                    w_glu=rows(g['w_s5_glu']))

    def loss_fn(diff):
        r = diff['rep']
        wgate = _unshard_cols(gates['w_gla_gate'] + diff['gates']['w_gla_gate'])
        wgate = jnp.pad(wgate, ((0, 0), (0, 0), (0, LANE - GLA_RANK), (0, 0)))
        bgate = _unshard_cols(gates['b_gla_gate'] + diff['gates']['b_gla_gate'])[:, :, None, :]

        c16 = jnp.concatenate([c_all, r['c_ctx'][None], jnp.zeros((7, dm), F32)], 0)
        c16s, = ops['silu_c']((c16,), ())

        def small(l):
            disc = jax.vmap(_s5_discretise)(r['s5_lam_re'][l], r['s5_lam_im'][l], r['s5_log_dt'][l])
            a_re, a_im = (z.reshape(2, S5_ROWS, LANE) for z in disc[:2])
            k_re, k_im = (z.reshape(2, 1, S5_CH) for z in disc[2:])
            b_full = jnp.concatenate([_block_diag_in(r['s5_b_re'][l]), _block_diag_in(r['s5_b_im'][l])], -1)
            c_re, c_im = r['s5_c_re'][l], r['s5_c_im'][l]
            c_full = [_block_diag_out(c_re[0]), -_block_diag_out(c_im[0]),
                      _block_diag_out(c_re[1]), -_block_diag_out(c_im[1])]
            row = lambda k: r[k][l][None, :]
            return dict(w_ada=w_ada_bf[l], p_ada=diff['ada'][l], b_ada=r['b_ada'][l], wgate=wgate[l], bgate=bgate[l],
                        a_re=a_re, a_im=a_im, k_re=k_re, k_im=k_im, b_full=b_full, c_full=c_full,
                        qw=_deinterleave(r['q_norm_w'][l], 1)[None, :], kw=_deinterleave(r['k_norm_w'][l], 1)[None, :],
                        gnw=row('gla_norm_w'), s5_d=row('s5_d'), ln1_w=row('ln1_w'), ln1_b=row('ln1_b'),
                        ln2_w=row('ln2_w'), ln2_b=row('ln2_b'))

        family_of = {v: f for f, v in carried_by.items()}

        def matmul(k, a, p, nxt, got):
            if nxt is None or k not in family_of:
                return lin[k](a, p['w_' + k], p['p_' + k])
            fam = family_of[k]
            out, got['w'][fam], got['g'][fam] = carrier[k](a, p['w_' + k], p['p_' + k], nxt[0][fam], nxt[1][fam])
            return out

        def layer(xc, p, nxt, got):
            msh = lin['ada'](c16s, p['w_ada'], p['p_ada']) + lax.dynamic_slice(p['b_ada'], (me * ada_cols,), (ada_cols,))
            mods = gather_rows(msh)
            m_lat = lax.dynamic_index_in_dim(mods, me, axis=1, keepdims=False).reshape(1, 6 * dm)
            m_ctx = mods[:, N_DEV, :].reshape(1, 6 * dm)
            sh1, sc1, g1, sh2, sc2, g2 = [(m_ctx[:, j * dm:(j + 1) * dm], m_lat[:, j * dm:(j + 1) * dm]) for j in range(6)]

            h, = ops['modulate1']((xc,), sh1 + sc1)
            gq, gk, gv, gr, su, aq, ak, av, bgt, glr = split_in(matmul('in', h, p, nxt, got))
            o2 = gla(gq, gk, gv, glr, p['wgate'], p['bgate'])
            o_gla, = ops['gla_out']((o2[0], o2[1], gr), (p['gnw'],))

            bu = lin['s5b'](su, p['b_full'].astype(BF16), p['b_full'])
            xin = ops['s5_in']((bu,), (p['k_re'][0], p['k_im'][0], p['k_re'][1], p['k_im'][1]))
            tile3 = lambda z: z.reshape(n, S5_ROWS, LANE)
            st = []
            for d in range(2):
                st += scans[d](tile3(xin[2 * d]), tile3(xin[2 * d + 1]), p['a_re'][d], p['a_im'][d])
            y0 = sum(lin['s5c%d' % j](z.reshape(n, S5_CH), cm.astype(BF16), cm)
                     for j, (z, cm) in enumerate(zip(st, p['c_full'])))
            ya, = ops['s5_mid']((y0, su), (p['s5_d'],))
            o_s5, = ops['s5_gate']((ya, matmul('glu', ya, p, nxt, got)), ())

            q, k = ops['qk_prep']((aq, ak, cosf, sinf), (p['qw'], p['kw']))
            o_att = attention(q, k, av)

            merged, = ops['merge']((matmul('pg', o_gla, p, nxt, got), matmul('ps', o_s5, p, nxt, got),
                                    matmul('pa', o_att, p, nxt, got), bgt), ())
            mix = matmul('out', merged, p, nxt, got)
            x1, = ops['postnorm1']((xc, mix), g1 + (p['ln1_w'], p['ln1_b']))
            h2, = ops['modulate2']((x1,), sh2 + sc2)
            act, = ops['swiglu']((matmul('fa', h2, p, nxt, got), matmul('fb', h2, p, nxt, got)), ())
            x2, = ops['postnorm2']((x1, matmul('fo', act, p, nxt, got)), g2 + (p['ln2_w'], p['ln2_b']))
            return x2

        xc = jnp.concatenate([ctx, diff['x']], 0)
        weights, stand_ins = gathered0, diff['grads'][0]
        for l in range(depth):
            p = small(l)
            p.update(unpack(weights))
            p.update({'p' + k[1:]: v for k, v in unpack({k: v.astype(F32) for k, v in stand_ins.items()}).items()})
            nxt = (blocks[l + 1], diff['grads'][l + 1]) if l + 1 < depth else None
            got = {'w': {}, 'g': {}}
            xc = layer(xc, p, nxt, got)
            weights, stand_ins = {k: lax.stop_gradient(v) for k, v in got['w'].items()}, got['g']
        return loss_op(xc[n_ctx:], target)

    loss_local, grads = jax.value_and_grad(loss_fn)(diff0)
    loss = lax.psum(loss_local, ("x", "y", "c"))

    out = {}

    def update(k, parts):
        w2 = _flat2d(given[k])
        res = adamw_call(parts.reshape((parts.shape[0],) + w2.shape), w2, _flat2d(given['m_' + k]),
                         _flat2d(given['v_' + k]), 'adamw_' + k)
        out[k] = [z.reshape(given[k].shape) for z in res]

    parts = {k: jnp.stack([reduce_scatter_parts(grads['grads'][0][k], 'scatter_' + k)] +
                          [grads['grads'][l][k] for l in range(1, depth)], axis=1) for k in FAMILIES}
    parts['w_ffn_in'] = jnp.concatenate([parts.pop('w_ffn_in_top'), parts.pop('w_ffn_in_bottom')], axis=2)
    for k in parts:
        update(k, parts[k])
    for k in gates:
        update(k, all_to_all(grads['gates'][k], 'scatter_' + k))
    update('w_ada', grads['ada'][None])

    sizes = [int(np.prod(given[k].shape)) for k in REPLICATED]
    total = sum(sizes)
    wide = 8 * LANE
    padded = -(-total // (8 * wide)) * (8 * wide)

    def flat(prefix, src):
        v = jnp.concatenate([src[prefix + k].reshape(-1) for k in REPLICATED] + [jnp.zeros((padded - total,), F32)])
        return v.reshape(padded // wide, wide)

    g_all = all_gather(flat('', grads['rep']), 'gather_small_grads')
    res = adamw_call(g_all, flat('', given), flat('m_', given), flat('v_', given), 'adamw_small')
    offs = np.cumsum([0] + sizes)
    for j, k in enumerate(REPLICATED):
        out[k] = [z.reshape(-1)[offs[j]:offs[j + 1]].reshape(given[k].shape) for z in res]

    grad_x = grads['x'][None]
    return (loss, grad_x) + tuple(out[k][j] for j in range(4) for k in WEIGHTS)


def kernel(x, c, ctx, c_ctx, w_ada, b_ada, w_in, w_gla_gate, b_gla_gate, gla_norm_w, s5_lam_re, s5_lam_im, s5_log_dt, s5_b_re, s5_b_im, s5_c_re, s5_c_im, s5_d, w_s5_glu, q_norm_w, k_norm_w, w_proj_gla, w_proj_s5, w_proj_attn, w_out, ln1_w, ln1_b, ln2_w, ln2_b, w_ffn_in, w_ffn_out, loss_target, m_c_ctx, m_w_ada, m_b_ada, m_w_in, m_w_gla_gate, m_b_gla_gate, m_gla_norm_w, m_s5_lam_re, m_s5_lam_im, m_s5_log_dt, m_s5_b_re, m_s5_b_im, m_s5_c_re, m_s5_c_im, m_s5_d, m_w_s5_glu, m_q_norm_w, m_k_norm_w, m_w_proj_gla, m_w_proj_s5, m_w_proj_attn, m_w_out, m_ln1_w, m_ln1_b, m_ln2_w, m_ln2_b, m_w_ffn_in, m_w_ffn_out, v_c_ctx, v_w_ada, v_b_ada, v_w_in, v_w_gla_gate, v_b_gla_gate, v_gla_norm_w, v_s5_lam_re, v_s5_lam_im, v_s5_log_dt, v_s5_b_re, v_s5_b_im, v_s5_c_re, v_s5_c_im, v_s5_d, v_w_s5_glu, v_q_norm_w, v_k_norm_w, v_w_proj_gla, v_w_proj_s5, v_w_proj_attn, v_w_out, v_ln1_w, v_ln1_b, v_ln2_w, v_ln2_b, v_w_ffn_in, v_w_ffn_out):
    return _step(dict(locals()))
```
